```python
import math
import jax, jax.numpy as jnp
from jax import lax
import numpy as np

D_MODEL = 1024
BATCH = 8
SEQ = 8192
DEPTH = 1

CHUNK = 64

SSM_EXPAND = 2
D_INNER = SSM_EXPAND * D_MODEL
SSM_HEAD_DIM = 64
SSM_HEADS = D_INNER // SSM_HEAD_DIM
SSM_GROUPS = 8
SSM_HEADS_PER_GROUP = SSM_HEADS // SSM_GROUPS
D_STATE = 128
CONV_WIDTH = 4
CONV_DIM = D_INNER + 2 * SSM_GROUPS * D_STATE
SSM_NORM_GROUP = D_INNER // SSM_GROUPS

ATTN_HEAD_DIM = 64
ATTN_HEADS = D_MODEL // (2 * ATTN_HEAD_DIM)
ATTN_QK_DIM = ATTN_HEADS * 2 * ATTN_HEAD_DIM
ATTN_V_DIM = ATTN_HEADS * 2 * ATTN_HEAD_DIM
Q_BLOCK = 128

N_BRANCHES = 2
D_FF = 4 * D_MODEL
D_PROJ = D_INNER + CONV_DIM + SSM_HEADS + 2 * ATTN_QK_DIM + ATTN_V_DIM + N_BRANCHES * D_MODEL
EPS = 1e-5

kernel_name = "hybrid_ssd_diffattn_gated_block"


def rmsnorm(x, g):
    xf = x.astype(jnp.float32)
    y = xf * lax.rsqrt(jnp.mean(xf * xf, axis=-1, keepdims=True) + EPS)
    return (y * g.astype(jnp.float32)).astype(x.dtype)


def causal_depthwise_conv(u, w, b):
    c = u.shape[-1]
    out = lax.conv_general_dilated(
        u, w[:, None, :].astype(u.dtype), window_strides=(1,),
        padding=[(CONV_WIDTH - 1, 0)],
        dimension_numbers=("NWC", "WIO", "NWC"),
        feature_group_count=c)
    return out + b.astype(u.dtype)


def to_chunks(t):
    nc = t.shape[1] // CHUNK
    return jnp.moveaxis(t.reshape((t.shape[0], nc, CHUNK) + t.shape[2:]), 1, 0)


def ssd_chunked_scan(xs, dt, a, bm, cm):
    f32 = jnp.float32
    bsz = xs.shape[0]
    xs_c, dt_c, b_c, c_c = (to_chunks(t.astype(f32)) for t in (xs, dt, bm, cm))
    tri = jnp.tril(jnp.ones((CHUNK, CHUNK), dtype=bool))[None, :, :, None, None]

    def step(state, inp):
        xc, dtc, bc, cc = inp
        acs = jnp.cumsum(dtc * a, axis=1)
        seg = acs[:, :, None] - acs[:, None, :]
        decay = jnp.exp(jnp.where(tri, seg, -jnp.inf))
        cb = jnp.einsum("blgn,bsgn->blsg", cc, bc)
        w = cb[..., None] * decay * dtc[:, None]
        y_diag = jnp.einsum("blsgr,bsgrp->blgrp", w, xc)
        y_off = jnp.einsum("blgn,bgrpn->blgrp", cc, state) * jnp.exp(acs)[..., None]
        to_end = jnp.exp(acs[:, -1:] - acs) * dtc
        new_state = state * jnp.exp(acs[:, -1])[..., None, None] + jnp.einsum(
            "bsgn,bsgr,bsgrp->bgrpn", bc, to_end, xc)
        return new_state, y_diag + y_off

    state0 = jnp.zeros((bsz, SSM_GROUPS, SSM_HEADS_PER_GROUP, SSM_HEAD_DIM, D_STATE), f32)
    _, ys = lax.scan(step, state0, (xs_c, dt_c, b_c, c_c))
    ys = jnp.moveaxis(ys, 0, 1).reshape(xs.shape)
    return ys.astype(xs.dtype)


def diff_attention(q, k, v, lam):
    f32 = jnp.float32
    seq = q.shape[1]
    scale = ATTN_HEAD_DIM ** -0.5
    slopes = jnp.exp2(-8.0 * jnp.arange(1, ATTN_HEADS + 1, dtype=f32) / ATTN_HEADS)
    outs = []
    for i in range(seq // Q_BLOCK):
        q0, q1 = i * Q_BLOCK, (i + 1) * Q_BLOCK
        qb = q[:, q0:q1].astype(f32)
        kp = k[:, :q1].astype(f32)
        vp = v[:, :q1].astype(f32)
        qpos = jnp.arange(q0, q1)
        kpos = jnp.arange(0, q1)
        s = jnp.einsum("bqhmd,bkhmd->bhmqk", qb, kp) * scale
        dist = jnp.abs(qpos[:, None] - kpos[None, :]).astype(f32)
        alibi = -slopes[:, None, None] * dist[None]
        mask = (qpos[:, None] // CHUNK) >= (kpos[None, :] // CHUNK)
        s = jnp.where(mask, s + alibi[None, :, None], -jnp.inf)
        p = jax.nn.softmax(s, axis=-1)
        attn = p[:, :, 0] - lam * p[:, :, 1]
        outs.append(jnp.einsum("bhqk,bkhe->bqhe", attn, vp))
    return jnp.concatenate(outs, axis=1).astype(v.dtype)


def setup_inputs(seed: int = 0) -> dict:
    key = jax.random.key(seed)
    ks = jax.random.split(key, 24)
    f32 = jnp.float32

    def nrm(k, shape, scale):
        return jax.random.normal(k, shape, f32) * scale

    dt0 = jnp.exp(jax.random.uniform(ks[6], (DEPTH, SSM_HEADS), f32, math.log(1e-3), math.log(1e-1)))
    dt_bias = dt0 + jnp.log(-jnp.expm1(-dt0))
    a_log = jnp.log(jax.random.uniform(ks[7], (DEPTH, SSM_HEADS), f32, 1.0, 16.0))
    return {
        "x": nrm(ks[0], (BATCH, SEQ, D_MODEL), 1.0),
        "g_norm_mix": 1.0 + nrm(ks[1], (DEPTH, D_MODEL), 0.02),
        "w_in": nrm(ks[2], (DEPTH, D_MODEL, D_PROJ), D_MODEL ** -0.5),
        "b_gate": nrm(ks[3], (DEPTH, N_BRANCHES * D_MODEL), 0.02),
        "conv_w": nrm(ks[4], (DEPTH, CONV_WIDTH, CONV_DIM), CONV_WIDTH ** -0.5),
        "conv_b": nrm(ks[5], (DEPTH, CONV_DIM), 0.02),
        "dt_bias": dt_bias,
        "a_log": a_log,
        "d_skip": 1.0 + nrm(ks[8], (DEPTH, SSM_HEADS), 0.02),
        "g_ssm_norm": 1.0 + nrm(ks[9], (DEPTH, D_INNER), 0.02),
        "lambda_q1": nrm(ks[10], (DEPTH, ATTN_HEAD_DIM), 0.1),
        "lambda_k1": nrm(ks[11], (DEPTH, ATTN_HEAD_DIM), 0.1),
        "lambda_q2": nrm(ks[12], (DEPTH, ATTN_HEAD_DIM), 0.1),
        "lambda_k2": nrm(ks[13], (DEPTH, ATTN_HEAD_DIM), 0.1),
        "g_subln": 1.0 + nrm(ks[14], (DEPTH, 2 * ATTN_HEAD_DIM), 0.02),
        "w_br_ssm": nrm(ks[15], (DEPTH, D_INNER, D_MODEL), D_INNER ** -0.5),
        "w_br_attn": nrm(ks[16], (DEPTH, ATTN_V_DIM, D_MODEL), ATTN_V_DIM ** -0.5),
        "w_out": nrm(ks[17], (DEPTH, D_MODEL, D_MODEL), D_MODEL ** -0.5),
        "g_norm_mlp": 1.0 + nrm(ks[18], (DEPTH, D_MODEL), 0.02),
        "w_up": nrm(ks[19], (DEPTH, D_MODEL, D_FF), D_MODEL ** -0.5),
        "w_down": nrm(ks[20], (DEPTH, D_FF, D_MODEL), D_FF ** -0.5),
        "g_norm_final": 1.0 + nrm(ks[21], (D_MODEL,), 0.02),
    }


def reference(x, g_norm_mix, w_in, b_gate, conv_w, conv_b, dt_bias, a_log, d_skip, g_ssm_norm,
              lambda_q1, lambda_k1, lambda_q2, lambda_k2, g_subln, w_br_ssm, w_br_attn, w_out,
              g_norm_mlp, w_up, w_down, g_norm_final):
    f32 = jnp.float32
    bsz, seq, _ = x.shape
    split_points = list(np.cumsum([D_INNER, CONV_DIM, SSM_HEADS, ATTN_QK_DIM, ATTN_QK_DIM, ATTN_V_DIM]))
    for l in range(DEPTH):
        h = rmsnorm(x, g_norm_mix[l])
        proj = jnp.einsum("bsd,de->bse", h, w_in[l])
        z, xbc, dt_raw, q, k, v, gate_logits = jnp.split(proj, split_points, axis=-1)

        xbc = jax.nn.silu(causal_depthwise_conv(xbc, conv_w[l], conv_b[l]))
        xs, bm, cm = jnp.split(xbc, [D_INNER, D_INNER + SSM_GROUPS * D_STATE], axis=-1)
        xs = xs.reshape(bsz, seq, SSM_GROUPS, SSM_HEADS_PER_GROUP, SSM_HEAD_DIM)
        bm = bm.reshape(bsz, seq, SSM_GROUPS, D_STATE)
        cm = cm.reshape(bsz, seq, SSM_GROUPS, D_STATE)
        dt = jax.nn.softplus(dt_raw + dt_bias[l]).reshape(bsz, seq, SSM_GROUPS, SSM_HEADS_PER_GROUP)
        a = -jnp.exp(a_log[l].astype(f32)).reshape(SSM_GROUPS, SSM_HEADS_PER_GROUP)
        d_h = d_skip[l].reshape(SSM_GROUPS, SSM_HEADS_PER_GROUP)[..., None]
        y = ssd_chunked_scan(xs, dt, a, bm, cm) + d_h * xs
        y = y.reshape(bsz, seq, D_INNER) * jax.nn.silu(z)
        y = rmsnorm(y.reshape(bsz, seq, SSM_GROUPS, SSM_NORM_GROUP),
                    jnp.ones((SSM_NORM_GROUP,), f32)).reshape(bsz, seq, D_INNER) * g_ssm_norm[l]
        br_ssm = jnp.einsum("bse,ed->bsd", y, w_br_ssm[l])

        lam_init = 0.8 - 0.6 * math.exp(-0.3 * l)
        lam = (jnp.exp(jnp.sum(lambda_q1[l].astype(f32) * lambda_k1[l].astype(f32)))
               - jnp.exp(jnp.sum(lambda_q2[l].astype(f32) * lambda_k2[l].astype(f32))) + lam_init)
        qh = q.reshape(bsz, seq, ATTN_HEADS, 2, ATTN_HEAD_DIM)
        kh = k.reshape(bsz, seq, ATTN_HEADS, 2, ATTN_HEAD_DIM)
        vh = v.reshape(bsz, seq, ATTN_HEADS, 2 * ATTN_HEAD_DIM)
        o = diff_attention(qh, kh, vh, lam)
        o = rmsnorm(o, g_subln[l]) * (1.0 - lam_init)
        br_attn = jnp.einsum("bse,ed->bsd", o.reshape(bsz, seq, ATTN_V_DIM), w_br_attn[l])

        g_ssm, g_attn = jnp.split(jax.nn.sigmoid(gate_logits + b_gate[l]), 2, axis=-1)
        merged = g_ssm * br_ssm + g_attn * br_attn
        x = x + jnp.einsum("bsd,de->bse", merged, w_out[l])

        h = rmsnorm(x, g_norm_mlp[l])
        u = jax.nn.relu(jnp.einsum("bsd,df->bsf", h, w_up[l]))
        x = x + jnp.einsum("bsf,fd->bsd", u * u, w_down[l])
    return rmsnorm(x, g_norm_final)
```

```python
import functools
import math

import jax
import jax.numpy as jnp
from jax import lax
from jax.experimental import pallas as pl
from jax.experimental.pallas import tpu as pltpu

F32 = jnp.float32
BF16 = jnp.bfloat16

D_MODEL = 1024
CHUNK = 64
D_INNER = 2048
SSM_HEAD_DIM = 64
SSM_HEADS = 32
SSM_GROUPS = 8
HEADS_PER_GROUP = 4
D_STATE = 128
CONV_WIDTH = 4
CONV_DIM = 4096
GROUP_W = D_INNER // SSM_GROUPS
ATTN_HEADS = 8
ATTN_HEAD_DIM = 64
HEAD_W = 2 * ATTN_HEAD_DIM
D_FF = 4096
EPS = 1e-5
LOG2E = 1.4426950408889634
NEG_BIG = -1e30

LANES = 128

COL_XBC = 0
COL_Z = COL_XBC + CONV_DIM
COL_Q = COL_Z + D_INNER
COL_K = COL_Q + ATTN_HEADS * HEAD_W
COL_V = COL_K + ATTN_HEADS * HEAD_W
COL_GATE = COL_V + ATTN_HEADS * HEAD_W
N_SLAB = COL_GATE + 2 * D_MODEL

TM_PROJ = 1024
TN_PROJ = 1024
L_SSD = 128
T_ATTN = 512
TM_MERGE = 512
TM_MLP = 512
FF_CHUNK = 1024

VMEM_LIMIT = 56 * 1024 * 1024


def _nt_dot(a, b):
    return lax.dot_general(a, b, (((1,), (1,)), ((), ())), preferred_element_type=F32)


def _tn_dot(a, b):
    return lax.dot_general(a, b, (((0,), (0,)), ((), ())), preferred_element_type=F32)


def _dot(a, b):
    return jnp.dot(a, b, preferred_element_type=F32)


def _dot_f32(a, b):
    return jnp.dot(a, b, preferred_element_type=F32, precision=lax.Precision.HIGHEST)


def _sigmoid(x):
    return 1.0 / (1.0 + jnp.exp(-x))


def _inproj_kernel(x_ref, g_ref, w_ref, wdt_ref, dtb_ref, o_ref, dt_ref, h_ref):
    @pl.when(pl.program_id(1) == 0)
    def _():
        x = x_ref[...]
        ms = jnp.mean(x * x, axis=-1, keepdims=True)
        h = (x * lax.rsqrt(ms + EPS) * g_ref[...]).astype(BF16)
        h_ref[...] = h
        dt_raw = _dot(h, wdt_ref[...]) + dtb_ref[...]
        dt_ref[...] = jnp.maximum(dt_raw, 0.0) + jnp.log(1.0 + jnp.exp(-jnp.abs(dt_raw)))

    o_ref[...] = _dot(h_ref[...], w_ref[...]).astype(BF16)


def _inproj(x2, g, w_slab, w_dt, dt_bias):
    t = x2.shape[0]
    grid = (t // TM_PROJ, N_SLAB // TN_PROJ)
    return pl.pallas_call(
        _inproj_kernel,
        grid=grid,
        in_specs=[
            pl.BlockSpec((TM_PROJ, D_MODEL), lambda i, j: (i, 0)),
            pl.BlockSpec((1, D_MODEL), lambda i, j: (0, 0)),
            pl.BlockSpec((D_MODEL, TN_PROJ), lambda i, j: (0, j)),
            pl.BlockSpec((D_MODEL, LANES), lambda i, j: (0, 0)),
            pl.BlockSpec((1, LANES), lambda i, j: (0, 0)),
        ],
        out_specs=[
            pl.BlockSpec((TM_PROJ, TN_PROJ), lambda i, j: (i, j)),
            pl.BlockSpec((TM_PROJ, LANES), lambda i, j: (i, 0)),
        ],
        out_shape=[
            jax.ShapeDtypeStruct((t, N_SLAB), BF16),
            jax.ShapeDtypeStruct((t, LANES), F32),
        ],
        scratch_shapes=[pltpu.VMEM((TM_PROJ, D_MODEL), BF16)],
        compiler_params=pltpu.CompilerParams(
            dimension_semantics=("parallel", "arbitrary"),
            vmem_limit_bytes=VMEM_LIMIT),
        name="inproj",
    )(x2, g, w_slab, w_dt, dt_bias)


def _ssd_kernel(xbc_ref, z_ref, dt_ref, cw_ref, cb_ref, a_ref, dsk_ref, gn_ref, exp_ref,
                o_ref, xpad_ref, xc_ref, state_ref):
    L = L_SSD

    @pl.when(pl.program_id(1) == 0)
    def _():
        state_ref[...] = jnp.zeros_like(state_ref)
        xpad_ref[0:8, :] = jnp.zeros((8, CONV_DIM), F32)

    xpad_ref[8:8 + L, :] = xbc_ref[...].astype(F32)
    acc = jnp.broadcast_to(cb_ref[...], (L, CONV_DIM))
    for k in range(CONV_WIDTH):
        acc = acc + xpad_ref[pl.ds(8 - (CONV_WIDTH - 1) + k, L), :] * cw_ref[k:k + 1, :]
    xc_ref[...] = acc * _sigmoid(acc)
    xpad_ref[0:8, :] = xpad_ref[L:L + 8, :]

    dt = dt_ref[...]
    dta = dt * a_ref[...]
    row = lax.broadcasted_iota(jnp.int32, (L, L), 0)
    col = lax.broadcasted_iota(jnp.int32, (L, L), 1)
    tri = row >= col
    acs = _dot_f32(tri.astype(F32), dta)
    acs_t = acs.T
    acs_last = acs[L - 1:L, :]
    e_acs = jnp.exp(acs)
    to_end = jnp.exp(acs_last - acs) * dt
    wide = _dot_f32(jnp.concatenate([dt, e_acs, to_end], axis=0), exp_ref[...])

    for g in range(SSM_GROUPS):
        ch = slice(g * GROUP_W, (g + 1) * GROUP_W)
        xg = xc_ref[:, ch]
        bg = xc_ref[:, D_INNER + g * D_STATE:D_INNER + (g + 1) * D_STATE].astype(BF16)
        cg = xc_ref[:, D_INNER + SSM_GROUPS * D_STATE + g * D_STATE:
                    D_INNER + SSM_GROUPS * D_STATE + (g + 1) * D_STATE].astype(BF16)
        dt_w = wide[0:L, ch]
        ea_w = wide[L:2 * L, ch]
        te_w = wide[2 * L:3 * L, ch]
        cb = _nt_dot(cg, bg)
        xdt = xg * dt_w
        lane = lax.broadcasted_iota(jnp.int32, (L, GROUP_W), 1)
        y = xg * dsk_ref[:, ch]
        for r in range(HEADS_PER_GROUP):
            hd = g * HEADS_PER_GROUP + r
            seg = acs[:, hd:hd + 1] - acs_t[hd:hd + 1, :]
            w = (cb * jnp.where(tri, jnp.exp(seg), 0.0)).astype(BF16)
            in_head = (lane >= r * SSM_HEAD_DIM) & (lane < (r + 1) * SSM_HEAD_DIM)
            y = y + _dot(w, jnp.where(in_head, xdt, 0.0).astype(BF16))
        st = state_ref[g]
        y = y + _dot(cg, st.astype(BF16)) * ea_w
        state_ref[g] = st * ea_w[L - 1:L, :] + _tn_dot(bg, (xg * te_w).astype(BF16))

        z = z_ref[:, ch].astype(F32)
        y = y * (z * _sigmoid(z))
        ms = jnp.mean(y * y, axis=-1, keepdims=True)
        o_ref[:, ch] = (y * lax.rsqrt(ms + EPS) * gn_ref[:, ch]).astype(BF16)


def _ssd(proj, dt, conv_w, conv_b, a_row, dskip_w, g_norm, expand, bsz, seq):
    t = proj.shape[0]
    nc = seq // L_SSD
    const = lambda shape: pl.BlockSpec(shape, lambda b, c: (0, 0))
    return pl.pallas_call(
        _ssd_kernel,
        grid=(bsz, nc),
        in_specs=[
            pl.BlockSpec((L_SSD, CONV_DIM), lambda b, c: (b * nc + c, COL_XBC // CONV_DIM)),
            pl.BlockSpec((L_SSD, D_INNER), lambda b, c: (b * nc + c, COL_Z // D_INNER)),
            pl.BlockSpec((L_SSD, LANES), lambda b, c: (b * nc + c, 0)),
            const((CONV_WIDTH, CONV_DIM)),
            const((1, CONV_DIM)),
            const((1, LANES)),
            const((1, D_INNER)),
            const((1, D_INNER)),
            const((LANES, D_INNER)),
        ],
        out_specs=pl.BlockSpec((L_SSD, D_INNER), lambda b, c: (b * nc + c, 0)),
        out_shape=jax.ShapeDtypeStruct((t, D_INNER), BF16),
        scratch_shapes=[
            pltpu.VMEM((L_SSD + 8, CONV_DIM), F32),
            pltpu.VMEM((L_SSD, CONV_DIM), F32),
            pltpu.VMEM((SSM_GROUPS, D_STATE, GROUP_W), F32),
        ],
        compiler_params=pltpu.CompilerParams(
            dimension_semantics=("parallel", "arbitrary"),
            vmem_limit_bytes=VMEM_LIMIT),
        name="ssd",
    )(proj, proj, dt, conv_w, conv_b, a_row, dskip_w, g_norm, expand)


def _attn_kernel(slopes_ref, q_ref, k_ref, v_ref, lq1_ref, lk1_ref, lq2_ref, lk2_ref, gsub_ref,
                 o_ref, bias_off_ref, bias_diag_ref, m_ref, l_ref, acc_ref, *, lam_init):
    T = T_ATTN
    h = pl.program_id(1)
    i = pl.program_id(2)
    slope = slopes_ref[h] * LOG2E

    @pl.when(i == 0)
    def _():
        row = lax.broadcasted_iota(jnp.int32, (T, T), 0)
        col = lax.broadcasted_iota(jnp.int32, (T, T), 1)
        rel = (col - row).astype(F32)
        bias_off_ref[...] = slope * rel
        visible = (row // CHUNK) >= (col // CHUNK)
        bias_diag_ref[...] = jnp.where(visible, -slope * jnp.abs(rel), NEG_BIG)

    q = q_ref[...]
    qs = (q.astype(F32) * (ATTN_HEAD_DIM ** -0.5 * LOG2E)).astype(BF16)
    lane = lax.broadcasted_iota(jnp.int32, (T, HEAD_W), 1)
    zero = jnp.zeros_like(qs)
    q_maps = (jnp.where(lane < ATTN_HEAD_DIM, qs, zero), jnp.where(lane >= ATTN_HEAD_DIM, qs, zero))

    m_ref[...] = jnp.full(m_ref.shape, NEG_BIG, F32)
    l_ref[...] = jnp.zeros(l_ref.shape, F32)
    acc_ref[...] = jnp.zeros(acc_ref.shape, F32)

    def process(j, bias_ref, shift):
        start = pl.multiple_of(j * T, T)
        kj = k_ref[pl.ds(start, T), :]
        vj = v_ref[pl.ds(start, T), :]
        for mi in range(2):
            s = _nt_dot(q_maps[mi], kj) + bias_ref[...]
            m_old = m_ref[mi]
            m_new = jnp.maximum(m_old, jnp.max(s, axis=-1, keepdims=True) + shift)
            alpha = jnp.exp2(m_old - m_new)
            p = jnp.exp2(s - (m_new - shift))
            l_ref[mi] = alpha * l_ref[mi] + jnp.sum(p, axis=-1, keepdims=True)
            acc_ref[mi] = alpha * acc_ref[mi] + _dot(p.astype(BF16), vj)
            m_ref[mi] = m_new

    process(i, bias_diag_ref, 0.0)

    def body(j, carry):
        process(j, bias_off_ref, slope * ((j - i) * T).astype(F32))
        return carry

    lax.fori_loop(0, i, body, 0)

    lam = (jnp.exp(jnp.sum(lq1_ref[...] * lk1_ref[...], axis=-1, keepdims=True))
           - jnp.exp(jnp.sum(lq2_ref[...] * lk2_ref[...], axis=-1, keepdims=True)) + lam_init)
    o = acc_ref[0] / l_ref[0] - lam * (acc_ref[1] / l_ref[1])
    ms = jnp.mean(o * o, axis=-1, keepdims=True)
    o_ref[...] = (o * lax.rsqrt(ms + EPS) * gsub_ref[...] * (1.0 - lam_init)).astype(BF16)


def _attention(proj, slopes, lq1, lk1, lq2, lk2, g_subln, bsz, seq, lam_init):
    t = proj.shape[0]
    nq = seq // T_ATTN
    vec = lambda n: pl.BlockSpec((1, n), lambda b, h, i: (0, 0))
    return pl.pallas_call(
        functools.partial(_attn_kernel, lam_init=lam_init),
        grid=(bsz, ATTN_HEADS, nq),
        in_specs=[
            pl.BlockSpec(memory_space=pltpu.SMEM),
            pl.BlockSpec((T_ATTN, HEAD_W), lambda b, h, i: (b * nq + i, COL_Q // HEAD_W + h)),
            pl.BlockSpec((seq, HEAD_W), lambda b, h, i: (b, COL_K // HEAD_W + h)),
            pl.BlockSpec((seq, HEAD_W), lambda b, h, i: (b, COL_V // HEAD_W + h)),
            vec(ATTN_HEAD_DIM), vec(ATTN_HEAD_DIM), vec(ATTN_HEAD_DIM), vec(ATTN_HEAD_DIM),
            vec(HEAD_W),
        ],
        out_specs=pl.BlockSpec((T_ATTN, HEAD_W), lambda b, h, i: (b * nq + i, h)),
        out_shape=jax.ShapeDtypeStruct((t, ATTN_HEADS * HEAD_W), BF16),
        scratch_shapes=[
            pltpu.VMEM((T_ATTN, T_ATTN), F32),
            pltpu.VMEM((T_ATTN, T_ATTN), F32),
            pltpu.VMEM((2, T_ATTN, 1), F32),
            pltpu.VMEM((2, T_ATTN, 1), F32),
            pltpu.VMEM((2, T_ATTN, HEAD_W), F32),
        ],
        compiler_params=pltpu.CompilerParams(
            dimension_semantics=("parallel", "parallel", "arbitrary"),
            vmem_limit_bytes=VMEM_LIMIT),
        name="diff_attn",
    )(slopes, proj, proj, proj, lq1, lk1, lq2, lk2, g_subln)


def _merge_kernel(y_ref, o_ref, gs_ref, ga_ref, x_ref, bgs_ref, bga_ref, wbs_ref, wba_ref, wo_ref,
                  gn_ref, x1_ref, h2_ref):
    br_ssm = _dot(y_ref[...], wbs_ref[...])
    br_attn = _dot(o_ref[...], wba_ref[...])
    g_ssm = _sigmoid(gs_ref[...].astype(F32) + bgs_ref[...])
    g_attn = _sigmoid(ga_ref[...].astype(F32) + bga_ref[...])
    merged = (g_ssm * br_ssm + g_attn * br_attn).astype(BF16)
    x1 = x_ref[...] + _dot(merged, wo_ref[...])
    x1_ref[...] = x1
    ms = jnp.mean(x1 * x1, axis=-1, keepdims=True)
    h2_ref[...] = (x1 * lax.rsqrt(ms + EPS) * gn_ref[...]).astype(BF16)


def _merge(y_ssm, o_attn, proj, x2, bg_ssm, bg_attn, w_bs, w_ba, w_o, g_mlp):
    t = x2.shape[0]
    rows = lambda n: pl.BlockSpec((TM_MERGE, n), lambda i: (i, 0))
    const = lambda shape: pl.BlockSpec(shape, lambda i: (0, 0))
    return pl.pallas_call(
        _merge_kernel,
        grid=(t // TM_MERGE,),
        in_specs=[
            rows(D_INNER),
            rows(D_MODEL),
            pl.BlockSpec((TM_MERGE, D_MODEL), lambda i: (i, COL_GATE // D_MODEL)),
            pl.BlockSpec((TM_MERGE, D_MODEL), lambda i: (i, COL_GATE // D_MODEL + 1)),
            rows(D_MODEL),
            const((1, D_MODEL)), const((1, D_MODEL)),
            const((D_INNER, D_MODEL)), const((D_MODEL, D_MODEL)), const((D_MODEL, D_MODEL)),
            const((1, D_MODEL)),
        ],
        out_specs=[rows(D_MODEL), rows(D_MODEL)],
        out_shape=[jax.ShapeDtypeStruct((t, D_MODEL), F32), jax.ShapeDtypeStruct((t, D_MODEL), BF16)],
        compiler_params=pltpu.CompilerParams(
            dimension_semantics=("parallel",), vmem_limit_bytes=VMEM_LIMIT),
        name="merge",
    )(y_ssm, o_attn, proj, proj, x2, bg_ssm, bg_attn, w_bs, w_ba, w_o, g_mlp)


def _mlp_kernel(x1_ref, h2_ref, wu_ref, wd_ref, gf_ref, out_ref):
    h2 = h2_ref[...]
    acc = x1_ref[...]
    for c in range(D_FF // FF_CHUNK):
        u = jnp.maximum(_dot(h2, wu_ref[:, c * FF_CHUNK:(c + 1) * FF_CHUNK]), 0.0)
        acc = acc + _dot((u * u).astype(BF16), wd_ref[c * FF_CHUNK:(c + 1) * FF_CHUNK, :])
    ms = jnp.mean(acc * acc, axis=-1, keepdims=True)
    out_ref[...] = acc * lax.rsqrt(ms + EPS) * gf_ref[...]


def _mlp(x1, h2, w_up, w_down, g_final):
    t = x1.shape[0]
    rows = lambda n: pl.BlockSpec((TM_MLP, n), lambda i: (i, 0))
    const = lambda shape: pl.BlockSpec(shape, lambda i: (0, 0), pipeline_mode=pl.Buffered(1))
    return pl.pallas_call(
        _mlp_kernel,
        grid=(t // TM_MLP,),
        in_specs=[rows(D_MODEL), rows(D_MODEL), const((D_MODEL, D_FF)), const((D_FF, D_MODEL)),
                  const((1, D_MODEL))],
        out_specs=rows(D_MODEL),
        out_shape=jax.ShapeDtypeStruct((t, D_MODEL), F32),
        compiler_params=pltpu.CompilerParams(
            dimension_semantics=("parallel",), vmem_limit_bytes=VMEM_LIMIT),
        name="mlp",
    )(x1, h2, w_up, w_down, g_final)


def kernel(x, g_norm_mix, w_in, b_gate, conv_w, conv_b, dt_bias, a_log, d_skip, g_ssm_norm,
           lambda_q1, lambda_k1, lambda_q2, lambda_k2, g_subln, w_br_ssm, w_br_attn, w_out,
           g_norm_mlp, w_up, w_down, g_norm_final):
    bsz, seq, _ = x.shape
    depth = w_in.shape[0]
    x2 = x.reshape(bsz * seq, D_MODEL)

    expand = (jnp.arange(LANES)[:, None] == (jnp.arange(D_INNER)[None, :] // SSM_HEAD_DIM)).astype(F32)
    slopes = jnp.exp2(-8.0 * jnp.arange(1, ATTN_HEADS + 1, dtype=F32) / ATTN_HEADS)

    for l in range(depth):
        w = w_in[l]
        s_z, s_xbc, s_dt = D_INNER, D_INNER + CONV_DIM, D_INNER + CONV_DIM + SSM_HEADS
        w_slab = jnp.concatenate([w[:, s_z:s_xbc], w[:, :s_z], w[:, s_dt:]], axis=1).astype(BF16)
        w_dt = jnp.pad(w[:, s_xbc:s_dt], ((0, 0), (0, LANES - SSM_HEADS))).astype(BF16)
        dtb = jnp.pad(dt_bias[l], (0, LANES - SSM_HEADS)).reshape(1, LANES)
        a_row = jnp.pad(-jnp.exp(a_log[l].astype(F32)), (0, LANES - SSM_HEADS)).reshape(1, LANES)
        dskip_w = jnp.repeat(d_skip[l], SSM_HEAD_DIM).reshape(1, D_INNER)
        lam_init = 0.8 - 0.6 * math.exp(-0.3 * l)

        proj, dt = _inproj(x2, g_norm_mix[l].reshape(1, D_MODEL), w_slab, w_dt, dtb)
        y_ssm = _ssd(proj, dt, conv_w[l], conv_b[l].reshape(1, CONV_DIM), a_row, dskip_w,
                     g_ssm_norm[l].reshape(1, D_INNER), expand, bsz, seq)
        o_attn = _attention(proj, slopes,
                            lambda_q1[l].reshape(1, -1), lambda_k1[l].reshape(1, -1),
                            lambda_q2[l].reshape(1, -1), lambda_k2[l].reshape(1, -1),
                            g_subln[l].reshape(1, HEAD_W), bsz, seq, lam_init)
        x1, h2 = _merge(y_ssm, o_attn, proj, x2,
                        b_gate[l, :D_MODEL].reshape(1, D_MODEL), b_gate[l, D_MODEL:].reshape(1, D_MODEL),
                        w_br_ssm[l].astype(BF16), w_br_attn[l].astype(BF16), w_out[l].astype(BF16),
                        g_norm_mlp[l].reshape(1, D_MODEL))
        assert l == depth - 1
        x2 = _mlp(x1, h2, w_up[l].astype(BF16), w_down[l].astype(BF16),
                  g_norm_final.reshape(1, D_MODEL))
    return x2.reshape(bsz, seq, D_MODEL)
```

```python
import functools
import math

import jax
import jax.numpy as jnp
from jax import lax
from jax.experimental import pallas as pl
from jax.experimental.pallas import tpu as pltpu

F32 = jnp.float32
BF16 = jnp.bfloat16

D_MODEL = 1024
CHUNK = 64
D_INNER = 2048
SSM_HEAD_DIM = 64
SSM_HEADS = 32
SSM_GROUPS = 8
HEADS_PER_GROUP = 4
D_STATE = 128
CONV_WIDTH = 4
CONV_DIM = 4096
GROUP_W = D_INNER // SSM_GROUPS
ATTN_HEADS = 8
ATTN_HEAD_DIM = 64
HEAD_W = 2 * ATTN_HEAD_DIM
VT_ROWS = HEAD_W + 16
D_FF = 4096
EPS = 1e-5
LOG2E = 1.4426950408889634
NEG_BIG = -1e30

LANES = 128

COL_XBC = 0
COL_Z = COL_XBC + CONV_DIM
COL_Q = COL_Z + D_INNER
COL_K = COL_Q + ATTN_HEADS * HEAD_W
COL_V = COL_K + ATTN_HEADS * HEAD_W
COL_GATE = COL_V + ATTN_HEADS * HEAD_W
N_SLAB = COL_GATE + 2 * D_MODEL

TM_PROJ = 1024
TN_PROJ = 1024
L_SSD = 128
T_ATTN = 512
TM_MERGE = 512
TM_MLP = 512
FF_CHUNK = 1024

VMEM_LIMIT = 56 * 1024 * 1024


def _nt_dot(a, b):
    return lax.dot_general(a, b, (((1,), (1,)), ((), ())), preferred_element_type=F32)


def _tn_dot(a, b):
    return lax.dot_general(a, b, (((0,), (0,)), ((), ())), preferred_element_type=F32)


def _dot(a, b):
    return jnp.dot(a, b, preferred_element_type=F32)


def _dot_f32(a, b):
    return jnp.dot(a, b, preferred_element_type=F32, precision=lax.Precision.HIGHEST)


def _sigmoid(x):
    return 1.0 / (1.0 + jnp.exp(-x))


def _inproj_kernel(x_ref, g_ref, w_ref, wdt_ref, dtb_ref, o_ref, dt_ref, h_ref):
    @pl.when(pl.program_id(1) == 0)
    def _():
        x = x_ref[...]
        ms = jnp.mean(x * x, axis=-1, keepdims=True)
        h = (x * lax.rsqrt(ms + EPS) * g_ref[...]).astype(BF16)
        h_ref[...] = h
        dt_raw = _dot(h, wdt_ref[...]) + dtb_ref[...]
        dt_ref[...] = jnp.maximum(dt_raw, 0.0) + jnp.log(1.0 + jnp.exp(-jnp.abs(dt_raw)))

    o_ref[...] = _dot(h_ref[...], w_ref[...]).astype(BF16)


def _inproj(x2, g, w_slab, w_dt, dt_bias):
    t = x2.shape[0]
    grid = (t // TM_PROJ, N_SLAB // TN_PROJ)
    return pl.pallas_call(
        _inproj_kernel,
        grid=grid,
        in_specs=[
            pl.BlockSpec((TM_PROJ, D_MODEL), lambda i, j: (i, 0)),
            pl.BlockSpec((1, D_MODEL), lambda i, j: (0, 0)),
            pl.BlockSpec((D_MODEL, TN_PROJ), lambda i, j: (0, j)),
            pl.BlockSpec((D_MODEL, LANES), lambda i, j: (0, 0)),
            pl.BlockSpec((1, LANES), lambda i, j: (0, 0)),
        ],
        out_specs=[
            pl.BlockSpec((TM_PROJ, TN_PROJ), lambda i, j: (i, j)),
            pl.BlockSpec((TM_PROJ, LANES), lambda i, j: (i, 0)),
        ],
        out_shape=[
            jax.ShapeDtypeStruct((t, N_SLAB), BF16),
            jax.ShapeDtypeStruct((t, LANES), F32),
        ],
        scratch_shapes=[pltpu.VMEM((TM_PROJ, D_MODEL), BF16)],
        compiler_params=pltpu.CompilerParams(
            dimension_semantics=("parallel", "arbitrary"),
            vmem_limit_bytes=VMEM_LIMIT),
        name="inproj",
    )(x2, g, w_slab, w_dt, dt_bias)


def _ssd_kernel(xbc_ref, z_ref, dt_ref, cw_ref, cb_ref, a_ref, dsk_ref, gn_ref, exp_ref,
                o_ref, xpad_ref, xc_ref, state_ref):
    L = L_SSD

    @pl.when(pl.program_id(1) == 0)
    def _():
        state_ref[...] = jnp.zeros_like(state_ref)
        xpad_ref[0:8, :] = jnp.zeros((8, CONV_DIM), F32)

    xpad_ref[8:8 + L, :] = xbc_ref[...].astype(F32)
    acc = jnp.broadcast_to(cb_ref[...], (L, CONV_DIM))
    for k in range(CONV_WIDTH):
        acc = acc + xpad_ref[pl.ds(8 - (CONV_WIDTH - 1) + k, L), :] * cw_ref[k:k + 1, :]
    xc_ref[...] = acc * _sigmoid(acc)
    xpad_ref[0:8, :] = xpad_ref[L:L + 8, :]

    dt = dt_ref[...]
    dta = dt * a_ref[...]
    row = lax.broadcasted_iota(jnp.int32, (L, L), 0)
    col = lax.broadcasted_iota(jnp.int32, (L, L), 1)
    tri = row >= col
    acs = _dot_f32(tri.astype(F32), dta)
    acs_t = acs.T
    acs_last = acs[L - 1:L, :]
    e_acs = jnp.exp(acs)
    to_end = jnp.exp(acs_last - acs) * dt
    wide = _dot_f32(jnp.concatenate([dt, e_acs, to_end], axis=0), exp_ref[...])

    for g in range(SSM_GROUPS):
        ch = slice(g * GROUP_W, (g + 1) * GROUP_W)
        xg = xc_ref[:, ch]
        bg = xc_ref[:, D_INNER + g * D_STATE:D_INNER + (g + 1) * D_STATE].astype(BF16)
        cg = xc_ref[:, D_INNER + SSM_GROUPS * D_STATE + g * D_STATE:
                    D_INNER + SSM_GROUPS * D_STATE + (g + 1) * D_STATE].astype(BF16)
        dt_w = wide[0:L, ch]
        ea_w = wide[L:2 * L, ch]
        te_w = wide[2 * L:3 * L, ch]
        cb = _nt_dot(cg, bg)
        xdt = xg * dt_w
        lane = lax.broadcasted_iota(jnp.int32, (L, GROUP_W), 1)
        y = xg * dsk_ref[:, ch]
        for r in range(HEADS_PER_GROUP):
            hd = g * HEADS_PER_GROUP + r
            seg = acs[:, hd:hd + 1] - acs_t[hd:hd + 1, :]
            w = (cb * jnp.where(tri, jnp.exp(seg), 0.0)).astype(BF16)
            in_head = (lane >= r * SSM_HEAD_DIM) & (lane < (r + 1) * SSM_HEAD_DIM)
            y = y + _dot(w, jnp.where(in_head, xdt, 0.0).astype(BF16))
        st = state_ref[g]
        y = y + _dot(cg, st.astype(BF16)) * ea_w
        state_ref[g] = st * ea_w[L - 1:L, :] + _tn_dot(bg, (xg * te_w).astype(BF16))

        z = z_ref[:, ch].astype(F32)
        y = y * (z * _sigmoid(z))
        ms = jnp.mean(y * y, axis=-1, keepdims=True)
        o_ref[:, ch] = (y * lax.rsqrt(ms + EPS) * gn_ref[:, ch]).astype(BF16)


def _ssd(proj, dt, conv_w, conv_b, a_row, dskip_w, g_norm, expand, bsz, seq):
    t = proj.shape[0]
    nc = seq // L_SSD
    const = lambda shape: pl.BlockSpec(shape, lambda b, c: (0, 0))
    return pl.pallas_call(
        _ssd_kernel,
        grid=(bsz, nc),
        in_specs=[
            pl.BlockSpec((L_SSD, CONV_DIM), lambda b, c: (b * nc + c, COL_XBC // CONV_DIM)),
            pl.BlockSpec((L_SSD, D_INNER), lambda b, c: (b * nc + c, COL_Z // D_INNER)),
            pl.BlockSpec((L_SSD, LANES), lambda b, c: (b * nc + c, 0)),
            const((CONV_WIDTH, CONV_DIM)),
            const((1, CONV_DIM)),
            const((1, LANES)),
            const((1, D_INNER)),
            const((1, D_INNER)),
            const((LANES, D_INNER)),
        ],
        out_specs=pl.BlockSpec((L_SSD, D_INNER), lambda b, c: (b * nc + c, 0)),
        out_shape=jax.ShapeDtypeStruct((t, D_INNER), BF16),
        scratch_shapes=[
            pltpu.VMEM((L_SSD + 8, CONV_DIM), F32),
            pltpu.VMEM((L_SSD, CONV_DIM), F32),
            pltpu.VMEM((SSM_GROUPS, D_STATE, GROUP_W), F32),
        ],
        compiler_params=pltpu.CompilerParams(
            dimension_semantics=("parallel", "arbitrary"),
            vmem_limit_bytes=VMEM_LIMIT),
        name="ssd",
    )(proj, proj, dt, conv_w, conv_b, a_row, dskip_w, g_norm, expand)


def _attn_kernel(slopes_ref, q_ref, k_ref, v_ref, lq1_ref, lk1_ref, lq2_ref, lk2_ref, gsub_ref,
                 o_ref, vt_ref, bias_off_ref, bias_diag_ref, qm_ref, s_a, s_b, cm_a, cm_b, m_ref,
                 acc_ref, *, lam_init):
    T = T_ATTN
    h = pl.program_id(1)
    i = pl.program_id(2)
    slope = slopes_ref[h] * LOG2E

    @pl.when(i == 0)
    def _():
        key = lax.broadcasted_iota(jnp.int32, (T, T), 0)
        qry = lax.broadcasted_iota(jnp.int32, (T, T), 1)
        rel = (key - qry).astype(F32)
        bias_off_ref[...] = slope * rel
        visible = (qry // CHUNK) >= (key // CHUNK)
        bias_diag_ref[...] = jnp.where(visible, -slope * jnp.abs(rel), NEG_BIG)
        ones_rows = (lax.broadcasted_iota(jnp.int32, (VT_ROWS - HEAD_W, T), 0) == 0).astype(BF16)
        for c in range(vt_ref.shape[0]):
            vt_ref[c, 0:HEAD_W, :] = v_ref[c * T:(c + 1) * T, :].astype(F32).T.astype(BF16)
            vt_ref[c, HEAD_W:VT_ROWS, :] = ones_rows

    q = q_ref[...]
    qs = (q.astype(F32) * (ATTN_HEAD_DIM ** -0.5 * LOG2E)).astype(BF16)
    lane = lax.broadcasted_iota(jnp.int32, (T, HEAD_W), 1)
    zero = jnp.zeros_like(qs)
    qm_ref[0] = jnp.where(lane < ATTN_HEAD_DIM, qs, zero)
    qm_ref[1] = jnp.where(lane >= ATTN_HEAD_DIM, qs, zero)

    m_ref[...] = jnp.full(m_ref.shape, NEG_BIG, F32)
    acc_ref[...] = jnp.zeros(acc_ref.shape, F32)

    def scores(j, bias_ref, s_buf, cm_buf):
        kj = k_ref[pl.ds(pl.multiple_of(j * T, T), T), :]
        for mi in range(2):
            s = _nt_dot(kj, qm_ref[mi]) + bias_ref[...]
            s_buf[mi] = s
            cm_buf[mi] = jnp.max(s, axis=0, keepdims=True)

    def accumulate(j, s_buf, cm_buf, shift):
        vtj = vt_ref[j]
        for mi in range(2):
            m_old = m_ref[mi]
            m_new = jnp.maximum(m_old, cm_buf[mi] + shift)
            alpha = jnp.exp2(m_old - m_new)
            p = jnp.exp2(s_buf[mi] - (m_new - shift))
            acc_ref[mi] = alpha * acc_ref[mi] + _dot(vtj, p.astype(BF16))
            m_ref[mi] = m_new

    def off_shift(j):
        return slope * ((j - i) * T).astype(F32)

    scores(i, bias_diag_ref, s_a, cm_a)
    n_pairs = i // 2

    def body(t, carry):
        j_a, shift_a = carry
        scores(2 * t, bias_off_ref, s_b, cm_b)
        accumulate(j_a, s_a, cm_a, shift_a)
        scores(2 * t + 1, bias_off_ref, s_a, cm_a)
        accumulate(2 * t, s_b, cm_b, off_shift(2 * t))
        return 2 * t + 1, off_shift(2 * t + 1)

    j_a, shift_a = lax.fori_loop(0, n_pairs, body, (i, jnp.float32(0.0)))

    @pl.when(i % 2 == 1)
    def _():
        scores(i - 1, bias_off_ref, s_b, cm_b)
        accumulate(j_a, s_a, cm_a, shift_a)
        accumulate(i - 1, s_b, cm_b, off_shift(i - 1))

    @pl.when(i % 2 == 0)
    def _():
        accumulate(j_a, s_a, cm_a, shift_a)

    lam = (jnp.exp(jnp.sum(lq1_ref[...] * lk1_ref[...], axis=-1, keepdims=True))
           - jnp.exp(jnp.sum(lq2_ref[...] * lk2_ref[...], axis=-1, keepdims=True)) + lam_init)
    num = [acc_ref[mi, 0:HEAD_W, :] for mi in range(2)]
    den = [acc_ref[mi, HEAD_W:HEAD_W + 1, :] for mi in range(2)]
    o_t = num[0] * (1.0 / den[0]) - lam * (num[1] * (1.0 / den[1]))
    ms = jnp.mean(o_t * o_t, axis=0, keepdims=True)
    o = (o_t * lax.rsqrt(ms + EPS)).T
    o_ref[...] = (o * gsub_ref[...] * (1.0 - lam_init)).astype(BF16)


def _attention(proj, slopes, lq1, lk1, lq2, lk2, g_subln, bsz, seq, lam_init):
    t = proj.shape[0]
    nq = seq // T_ATTN
    vec = lambda n: pl.BlockSpec((1, n), lambda b, h, i: (0, 0))
    return pl.pallas_call(
        functools.partial(_attn_kernel, lam_init=lam_init),
        grid=(bsz, ATTN_HEADS, nq),
        in_specs=[
            pl.BlockSpec(memory_space=pltpu.SMEM),
            pl.BlockSpec((T_ATTN, HEAD_W), lambda b, h, i: (b * nq + i, COL_Q // HEAD_W + h)),
            pl.BlockSpec((seq, HEAD_W), lambda b, h, i: (b, COL_K // HEAD_W + h)),
            pl.BlockSpec((seq, HEAD_W), lambda b, h, i: (b, COL_V // HEAD_W + h)),
            vec(ATTN_HEAD_DIM), vec(ATTN_HEAD_DIM), vec(ATTN_HEAD_DIM), vec(ATTN_HEAD_DIM),
            vec(HEAD_W),
        ],
        out_specs=pl.BlockSpec((T_ATTN, HEAD_W), lambda b, h, i: (b * nq + i, h)),
        out_shape=jax.ShapeDtypeStruct((t, ATTN_HEADS * HEAD_W), BF16),
        scratch_shapes=[
            pltpu.VMEM((nq, VT_ROWS, T_ATTN), BF16),
            pltpu.VMEM((T_ATTN, T_ATTN), F32),
            pltpu.VMEM((T_ATTN, T_ATTN), F32),
            pltpu.VMEM((2, T_ATTN, HEAD_W), BF16),
            pltpu.VMEM((2, T_ATTN, T_ATTN), F32),
            pltpu.VMEM((2, T_ATTN, T_ATTN), F32),
            pltpu.VMEM((2, 1, T_ATTN), F32),
            pltpu.VMEM((2, 1, T_ATTN), F32),
            pltpu.VMEM((2, 1, T_ATTN), F32),
            pltpu.VMEM((2, VT_ROWS, T_ATTN), F32),
        ],
        compiler_params=pltpu.CompilerParams(
            dimension_semantics=("parallel", "parallel", "arbitrary"),
            vmem_limit_bytes=VMEM_LIMIT),
        name="diff_attn",
    )(slopes, proj, proj, proj, lq1, lk1, lq2, lk2, g_subln)


def _merge_kernel(y_ref, o_ref, gs_ref, ga_ref, x_ref, bgs_ref, bga_ref, wbs_ref, wba_ref, wo_ref,
                  gn_ref, x1_ref, h2_ref):
    br_ssm = _dot(y_ref[...], wbs_ref[...])
    br_attn = _dot(o_ref[...], wba_ref[...])
    g_ssm = _sigmoid(gs_ref[...].astype(F32) + bgs_ref[...])
    g_attn = _sigmoid(ga_ref[...].astype(F32) + bga_ref[...])
    merged = (g_ssm * br_ssm + g_attn * br_attn).astype(BF16)
    x1 = x_ref[...] + _dot(merged, wo_ref[...])
    x1_ref[...] = x1
    ms = jnp.mean(x1 * x1, axis=-1, keepdims=True)
    h2_ref[...] = (x1 * lax.rsqrt(ms + EPS) * gn_ref[...]).astype(BF16)


def _merge(y_ssm, o_attn, proj, x2, bg_ssm, bg_attn, w_bs, w_ba, w_o, g_mlp):
    t = x2.shape[0]
    rows = lambda n: pl.BlockSpec((TM_MERGE, n), lambda i: (i, 0))
    const = lambda shape: pl.BlockSpec(shape, lambda i: (0, 0))
    return pl.pallas_call(
        _merge_kernel,
        grid=(t // TM_MERGE,),
        in_specs=[
            rows(D_INNER),
            rows(D_MODEL),
            pl.BlockSpec((TM_MERGE, D_MODEL), lambda i: (i, COL_GATE // D_MODEL)),
            pl.BlockSpec((TM_MERGE, D_MODEL), lambda i: (i, COL_GATE // D_MODEL + 1)),
            rows(D_MODEL),
            const((1, D_MODEL)), const((1, D_MODEL)),
            const((D_INNER, D_MODEL)), const((D_MODEL, D_MODEL)), const((D_MODEL, D_MODEL)),
            const((1, D_MODEL)),
        ],
        out_specs=[rows(D_MODEL), rows(D_MODEL)],
        out_shape=[jax.ShapeDtypeStruct((t, D_MODEL), F32), jax.ShapeDtypeStruct((t, D_MODEL), BF16)],
        compiler_params=pltpu.CompilerParams(
            dimension_semantics=("parallel",), vmem_limit_bytes=VMEM_LIMIT),
        name="merge",
    )(y_ssm, o_attn, proj, proj, x2, bg_ssm, bg_attn, w_bs, w_ba, w_o, g_mlp)


def _mlp_kernel(x1_ref, h2_ref, wu_ref, wd_ref, gf_ref, out_ref):
    h2 = h2_ref[...]
    acc = x1_ref[...]
    for c in range(D_FF // FF_CHUNK):
        u = jnp.maximum(_dot(h2, wu_ref[:, c * FF_CHUNK:(c + 1) * FF_CHUNK]), 0.0)
        acc = acc + _dot((u * u).astype(BF16), wd_ref[c * FF_CHUNK:(c + 1) * FF_CHUNK, :])
    ms = jnp.mean(acc * acc, axis=-1, keepdims=True)
    out_ref[...] = acc * lax.rsqrt(ms + EPS) * gf_ref[...]


def _mlp(x1, h2, w_up, w_down, g_final):
    t = x1.shape[0]
    rows = lambda n: pl.BlockSpec((TM_MLP, n), lambda i: (i, 0))
    const = lambda shape: pl.BlockSpec(shape, lambda i: (0, 0), pipeline_mode=pl.Buffered(1))
    return pl.pallas_call(
        _mlp_kernel,
        grid=(t // TM_MLP,),
        in_specs=[rows(D_MODEL), rows(D_MODEL), const((D_MODEL, D_FF)), const((D_FF, D_MODEL)),
                  const((1, D_MODEL))],
        out_specs=rows(D_MODEL),
        out_shape=jax.ShapeDtypeStruct((t, D_MODEL), F32),
        compiler_params=pltpu.CompilerParams(
            dimension_semantics=("parallel",), vmem_limit_bytes=VMEM_LIMIT),
        name="mlp",
    )(x1, h2, w_up, w_down, g_final)


def kernel(x, g_norm_mix, w_in, b_gate, conv_w, conv_b, dt_bias, a_log, d_skip, g_ssm_norm,
           lambda_q1, lambda_k1, lambda_q2, lambda_k2, g_subln, w_br_ssm, w_br_attn, w_out,
           g_norm_mlp, w_up, w_down, g_norm_final):
    bsz, seq, _ = x.shape
    depth = w_in.shape[0]
    x2 = x.reshape(bsz * seq, D_MODEL)

    expand = (jnp.arange(LANES)[:, None] == (jnp.arange(D_INNER)[None, :] // SSM_HEAD_DIM)).astype(F32)
    slopes = jnp.exp2(-8.0 * jnp.arange(1, ATTN_HEADS + 1, dtype=F32) / ATTN_HEADS)

    for l in range(depth):
        w = w_in[l]
        s_z, s_xbc, s_dt = D_INNER, D_INNER + CONV_DIM, D_INNER + CONV_DIM + SSM_HEADS
        w_slab = jnp.concatenate([w[:, s_z:s_xbc], w[:, :s_z], w[:, s_dt:]], axis=1).astype(BF16)
        w_dt = jnp.pad(w[:, s_xbc:s_dt], ((0, 0), (0, LANES - SSM_HEADS))).astype(BF16)
        dtb = jnp.pad(dt_bias[l], (0, LANES - SSM_HEADS)).reshape(1, LANES)
        a_row = jnp.pad(-jnp.exp(a_log[l].astype(F32)), (0, LANES - SSM_HEADS)).reshape(1, LANES)
        dskip_w = jnp.repeat(d_skip[l], SSM_HEAD_DIM).reshape(1, D_INNER)
        lam_init = 0.8 - 0.6 * math.exp(-0.3 * l)

        proj, dt = _inproj(x2, g_norm_mix[l].reshape(1, D_MODEL), w_slab, w_dt, dtb)
        y_ssm = _ssd(proj, dt, conv_w[l], conv_b[l].reshape(1, CONV_DIM), a_row, dskip_w,
                     g_ssm_norm[l].reshape(1, D_INNER), expand, bsz, seq)
        o_attn = _attention(proj, slopes,
                            lambda_q1[l].reshape(1, -1), lambda_k1[l].reshape(1, -1),
                            lambda_q2[l].reshape(1, -1), lambda_k2[l].reshape(1, -1),
                            g_subln[l].reshape(1, HEAD_W), bsz, seq, lam_init)
        x1, h2 = _merge(y_ssm, o_attn, proj, x2,
                        b_gate[l, :D_MODEL].reshape(1, D_MODEL), b_gate[l, D_MODEL:].reshape(1, D_MODEL),
                        w_br_ssm[l].astype(BF16), w_br_attn[l].astype(BF16), w_out[l].astype(BF16),
                        g_norm_mlp[l].reshape(1, D_MODEL))
        assert l == depth - 1
        x2 = _mlp(x1, h2, w_up[l].astype(BF16), w_down[l].astype(BF16),
                  g_norm_final.reshape(1, D_MODEL))
    return x2.reshape(bsz, seq, D_MODEL)
```

```python
import functools
import math

import jax
import jax.numpy as jnp
from jax import lax
from jax.experimental import pallas as pl
from jax.experimental.pallas import tpu as pltpu

F32 = jnp.float32
BF16 = jnp.bfloat16

D_MODEL = 1024
CHUNK = 64
D_INNER = 2048
SSM_HEAD_DIM = 64
SSM_HEADS = 32
SSM_GROUPS = 8
HEADS_PER_GROUP = 4
D_STATE = 128
CONV_WIDTH = 4
CONV_DIM = 4096
GROUP_W = D_INNER // SSM_GROUPS
ATTN_HEADS = 8
ATTN_HEAD_DIM = 64
HEAD_W = 2 * ATTN_HEAD_DIM
VT_ROWS = HEAD_W + 16
D_FF = 4096
EPS = 1e-5
LOG2E = 1.4426950408889634
NEG_BIG = -1e30

LANES = 128
N_SLABS = CONV_DIM // LANES
HIST = 8

COL_XBC = 0
COL_Z = COL_XBC + CONV_DIM
COL_Q = COL_Z + D_INNER
COL_K = COL_Q + ATTN_HEADS * HEAD_W
COL_V = COL_K + ATTN_HEADS * HEAD_W
COL_GATE = COL_V + ATTN_HEADS * HEAD_W
N_SLAB = COL_GATE + 2 * D_MODEL

TM_PROJ = 1024
TN_PROJ = 1024
L_SSD = 128
T_ATTN = 512
TM_MERGE = 512
TM_MLP = 512
FF_CHUNK = 1024

VMEM_LIMIT = 56 * 1024 * 1024


def _nt_dot(a, b):
    return lax.dot_general(a, b, (((1,), (1,)), ((), ())), preferred_element_type=F32)


def _dot(a, b):
    return jnp.dot(a, b, preferred_element_type=F32)


def _dot_f32(a, b):
    return jnp.dot(a, b, preferred_element_type=F32, precision=lax.Precision.HIGHEST)


def _sigmoid(x):
    return 1.0 / (1.0 + jnp.exp(-x))


def _inproj_kernel(x_ref, g_ref, w_ref, wdt_ref, dtb_ref, o_ref, dt_ref, h_ref):
    @pl.when(pl.program_id(1) == 0)
    def _():
        x = x_ref[...]
        ms = jnp.mean(x * x, axis=-1, keepdims=True)
        h = (x * lax.rsqrt(ms + EPS) * g_ref[...]).astype(BF16)
        h_ref[...] = h
        dt_raw = _dot(h, wdt_ref[...]) + dtb_ref[...]
        dt_ref[...] = jnp.maximum(dt_raw, 0.0) + jnp.log(1.0 + jnp.exp(-jnp.abs(dt_raw)))

    o_ref[...] = _dot(h_ref[...], w_ref[...]).astype(BF16)


def _inproj(x2, g, w_slab, w_dt, dt_bias):
    t = x2.shape[0]
    grid = (t // TM_PROJ, N_SLAB // TN_PROJ)
    return pl.pallas_call(
        _inproj_kernel,
        grid=grid,
        in_specs=[
            pl.BlockSpec((TM_PROJ, D_MODEL), lambda i, j: (i, 0)),
            pl.BlockSpec((1, D_MODEL), lambda i, j: (0, 0)),
            pl.BlockSpec((D_MODEL, TN_PROJ), lambda i, j: (0, j)),
            pl.BlockSpec((D_MODEL, LANES), lambda i, j: (0, 0)),
            pl.BlockSpec((1, LANES), lambda i, j: (0, 0)),
        ],
        out_specs=[
            pl.BlockSpec((TM_PROJ, TN_PROJ), lambda i, j: (i, j)),
            pl.BlockSpec((TM_PROJ, LANES), lambda i, j: (i, 0)),
        ],
        out_shape=[
            jax.ShapeDtypeStruct((t, N_SLAB), BF16),
            jax.ShapeDtypeStruct((t, LANES), F32),
        ],
        scratch_shapes=[pltpu.VMEM((TM_PROJ, D_MODEL), BF16)],
        compiler_params=pltpu.CompilerParams(
            dimension_semantics=("parallel", "arbitrary"),
            vmem_limit_bytes=VMEM_LIMIT),
        name="inproj",
    )(x2, g, w_slab, w_dt, dt_bias)


def _silu_of_half(hx):
    return hx + hx * jnp.tanh(hx)


def _ssd_kernel(xbc_ref, z_ref, dt_ref, cw_ref, cb_ref, a_ref, dsk_ref, gn_ref, sel_ref, selw_ref,
                hm_ref, o_ref, xpad_ref, xc_ref, state_ref):
    L = L_SSD
    assert L == D_STATE

    @pl.when(pl.program_id(1) == 0)
    def _():
        state_ref[...] = jnp.zeros_like(state_ref)
        xpad_ref[:, 0:HIST, :] = jnp.zeros((N_SLABS, HIST, LANES), F32)

    for c in range(N_SLABS):
        xpad_ref[c, HIST:HIST + L, :] = xbc_ref[:, c * LANES:(c + 1) * LANES].astype(F32)
    for c in range(N_SLABS):
        taps = [cw_ref[k, c] for k in range(CONV_WIDTH)]
        bias = cb_ref[c]
        for s in [32 * b + j for b in range(L // 32) for j in range(4)]:
            acc = bias
            for k in range(CONV_WIDTH):
                first = HIST + s - (CONV_WIDTH - 1) + k
                acc = acc + xpad_ref[pl.ds(c, 1), pl.ds(first, 8, stride=4), :][0] * taps[k]
            xc_ref[pl.ds(c, 1), pl.ds(s, 8, stride=4), :] = _silu_of_half(acc)[None]
    xpad_ref[:, 0:HIST, :] = xpad_ref[:, L:L + HIST, :]

    dt = dt_ref[...]
    row = lax.broadcasted_iota(jnp.int32, (L, L), 0)
    col = lax.broadcasted_iota(jnp.int32, (L, L), 1)
    tri = row >= col
    acs = _dot_f32(tri.astype(F32), dt * a_ref[...])
    acs_last = acs[L - 1:L, :]
    src_t = (acs - jnp.log2(dt)).T
    to_end_t = jnp.exp2(acs_last - acs + jnp.log2(dt)).T

    a1 = acs.astype(BF16)
    r1 = acs - a1.astype(F32)
    a2 = r1.astype(BF16)
    a3 = (r1 - a2.astype(F32)).astype(BF16)
    quarter = lax.broadcasted_iota(jnp.int32, (L, LANES), 1) // SSM_HEADS
    pieces = jnp.where(quarter == 0, a1, jnp.where(quarter == 1, a2, jnp.where(quarter == 2, a3, 0)))
    acs_bc = _dot(pieces, sel_ref[...])
    chunk_decay = jnp.exp2(_dot(pieces[L - 8:L, :], selw_ref[...]))[7:8, :]

    for g in range(SSM_GROUPS):
        ch = slice(g * GROUP_W, (g + 1) * GROUP_W)
        xg = jnp.concatenate([xc_ref[2 * g], xc_ref[2 * g + 1]], axis=1)
        bg = xc_ref[D_INNER // LANES + g]
        cg = xc_ref[D_INNER // LANES + SSM_GROUPS + g]
        cb = _nt_dot(cg.astype(BF16), bg.astype(BF16))
        bg_t = bg.T
        st = state_ref[g]
        xs = jnp.concatenate([xg.astype(BF16), st.astype(BF16)], axis=0)
        lhs, rhs, bts = [], [], []
        for r in range(HEADS_PER_GROUP):
            hd = g * HEADS_PER_GROUP + r
            bc = acs_bc[:, hd * L:(hd + 1) * L]
            w = cb * jnp.where(tri, jnp.exp2(bc - src_t[hd:hd + 1, :]), 0.0)
            cs = cg * jnp.exp2(bc)
            lhs += [w.astype(BF16), cs.astype(BF16)]
            rhs.append(xs * hm_ref[r])
            bts.append((bg_t * to_end_t[hd:hd + 1, :]).astype(BF16))
        rhs_all = jnp.concatenate(rhs, axis=0)
        y = xg * dsk_ref[:, ch] + _dot(jnp.concatenate(lhs, axis=1), rhs_all)
        x_heads = jnp.concatenate([rh[0:L, :] for rh in rhs], axis=0)
        state_ref[g] = st * chunk_decay[:, ch] + _dot(jnp.concatenate(bts, axis=1), x_heads)

        y = y * _silu_of_half(z_ref[:, ch].astype(F32))
        ms = jnp.mean(y * y, axis=-1, keepdims=True)
        o_ref[:, ch] = (y * lax.rsqrt(ms + EPS) * gn_ref[:, ch]).astype(BF16)


def _ssd(proj, dt, conv_w, conv_b, a_row, dskip_w, g_norm, sel, sel_wide, head_mask, bsz, seq):
    t = proj.shape[0]
    nc = seq // L_SSD
    const = lambda shape: pl.BlockSpec(shape, lambda b, c: (0,) * len(shape))
    return pl.pallas_call(
        _ssd_kernel,
        grid=(bsz, nc),
        in_specs=[
            pl.BlockSpec((L_SSD, CONV_DIM), lambda b, c: (b * nc + c, COL_XBC // CONV_DIM)),
            pl.BlockSpec((L_SSD, D_INNER), lambda b, c: (b * nc + c, COL_Z // D_INNER)),
            pl.BlockSpec((L_SSD, LANES), lambda b, c: (b * nc + c, 0)),
            const((CONV_WIDTH, N_SLABS, 1, LANES)),
            const((N_SLABS, 1, LANES)),
            const((1, LANES)),
            const((1, D_INNER)),
            const((1, D_INNER)),
            const((LANES, SSM_HEADS * L_SSD)),
            const((LANES, D_INNER)),
            const((HEADS_PER_GROUP, 1, GROUP_W)),
        ],
        out_specs=pl.BlockSpec((L_SSD, D_INNER), lambda b, c: (b * nc + c, 0)),
        out_shape=jax.ShapeDtypeStruct((t, D_INNER), BF16),
        scratch_shapes=[
            pltpu.VMEM((N_SLABS, L_SSD + HIST, LANES), F32),
            pltpu.VMEM((N_SLABS, L_SSD, LANES), F32),
            pltpu.VMEM((SSM_GROUPS, D_STATE, GROUP_W), F32),
        ],
        compiler_params=pltpu.CompilerParams(
            dimension_semantics=("parallel", "arbitrary"),
            vmem_limit_bytes=VMEM_LIMIT),
        name="ssd",
    )(proj, proj, dt, conv_w, conv_b, a_row, dskip_w, g_norm, sel, sel_wide, head_mask)


def _attn_kernel(slopes_ref, q_ref, k_ref, v_ref, lq1_ref, lk1_ref, lq2_ref, lk2_ref, gsub_ref,
                 o_ref, vt_ref, bias_off_ref, bias_diag_ref, qm_ref, s_a, s_b, cm_a, cm_b, m_ref,
                 acc_ref, *, lam_init):
    T = T_ATTN
    h = pl.program_id(1)
    step = pl.program_id(2)
    slope = slopes_ref[h] * LOG2E

    @pl.when(step == 0)
    def _():
        key = lax.broadcasted_iota(jnp.int32, (T, T), 0)
        qry = lax.broadcasted_iota(jnp.int32, (T, T), 1)
        rel = (key - qry).astype(F32)
        bias_off_ref[...] = slope * rel
        visible = (qry // CHUNK) >= (key // CHUNK)
        bias_diag_ref[...] = jnp.where(visible, -slope * jnp.abs(rel), NEG_BIG)
        ones_rows = (lax.broadcasted_iota(jnp.int32, (VT_ROWS - HEAD_W, T), 0) == 0).astype(BF16)
        for c in range(vt_ref.shape[0]):
            vt_ref[c, 0:HEAD_W, :] = v_ref[c * T:(c + 1) * T, :].astype(F32).T.astype(BF16)
            vt_ref[c, HEAD_W:VT_ROWS, :] = ones_rows

    lane = lax.broadcasted_iota(jnp.int32, (T, HEAD_W), 1)
    for tile in range(2):
        q = q_ref[tile * T:(tile + 1) * T, :]
        qs = (q.astype(F32) * (ATTN_HEAD_DIM ** -0.5 * LOG2E)).astype(BF16)
        zero = jnp.zeros_like(qs)
        qm_ref[2 * tile] = jnp.where(lane < ATTN_HEAD_DIM, qs, zero)
        qm_ref[2 * tile + 1] = jnp.where(lane >= ATTN_HEAD_DIM, qs, zero)

    m_ref[...] = jnp.full(m_ref.shape, NEG_BIG, F32)
    acc_ref[...] = jnp.zeros(acc_ref.shape, F32)

    def scores(j, tile, bias_ref, s_buf, cm_buf):
        kj = k_ref[pl.ds(pl.multiple_of(j * T, T), T), :]
        for mi in range(2):
            s = _nt_dot(kj, qm_ref[2 * tile + mi]) + bias_ref[...]
            s_buf[mi] = s
            cm_buf[mi] = jnp.max(s, axis=0, keepdims=True)

    def accumulate(j, tile, s_buf, cm_buf, shift):
        vtj = vt_ref[j]
        for mi in range(2):
            idx = 2 * tile + mi
            m_old = m_ref[idx]
            m_new = jnp.maximum(m_old, cm_buf[mi] + shift)
            alpha = jnp.exp2(m_old - m_new)
            p = jnp.exp2(s_buf[mi] - (m_new - shift))
            acc_ref[idx] = alpha * acc_ref[idx] + _dot(vtj, p.astype(BF16))
            m_ref[idx] = m_new

    def off_shift(j, tile):
        return slope * ((j - (2 * step + tile)) * T).astype(F32)

    d0 = 2 * step
    scores(d0, 0, bias_diag_ref, s_a, cm_a)
    scores(d0 + 1, 1, bias_diag_ref, s_b, cm_b)
    accumulate(d0, 0, s_a, cm_a, 0.0)

    def body(j, carry):
        j_b, shift_b = carry
        scores(j, 0, bias_off_ref, s_a, cm_a)
        accumulate(j_b, 1, s_b, cm_b, shift_b)
        scores(j, 1, bias_off_ref, s_b, cm_b)
        accumulate(j, 0, s_a, cm_a, off_shift(j, 0))
        return j, off_shift(j, 1)

    j_b, shift_b = lax.fori_loop(0, d0, body, (d0 + 1, jnp.float32(0.0)))
    scores(d0, 1, bias_off_ref, s_a, cm_a)
    accumulate(j_b, 1, s_b, cm_b, shift_b)
    accumulate(d0, 1, s_a, cm_a, off_shift(d0, 1))

    lam = (jnp.exp(jnp.sum(lq1_ref[...] * lk1_ref[...], axis=-1, keepdims=True))
           - jnp.exp(jnp.sum(lq2_ref[...] * lk2_ref[...], axis=-1, keepdims=True)) + lam_init)
    for tile in range(2):
        num = [acc_ref[2 * tile + mi, 0:HEAD_W, :] for mi in range(2)]
        den = [acc_ref[2 * tile + mi, HEAD_W:HEAD_W + 1, :] for mi in range(2)]
        o_t = num[0] * (1.0 / den[0]) - lam * (num[1] * (1.0 / den[1]))
        ms = jnp.mean(o_t * o_t, axis=0, keepdims=True)
        o = (o_t * lax.rsqrt(ms + EPS)).T
        o_ref[tile * T:(tile + 1) * T, :] = (o * gsub_ref[...] * (1.0 - lam_init)).astype(BF16)


def _attention(proj, slopes, lq1, lk1, lq2, lk2, g_subln, bsz, seq, lam_init):
    t = proj.shape[0]
    tq = 2 * T_ATTN
    nq = seq // tq
    vec = lambda n: pl.BlockSpec((1, n), lambda b, h, i: (0, 0))
    return pl.pallas_call(
        functools.partial(_attn_kernel, lam_init=lam_init),
        grid=(bsz, ATTN_HEADS, nq),
        in_specs=[
            pl.BlockSpec(memory_space=pltpu.SMEM),
            pl.BlockSpec((tq, HEAD_W), lambda b, h, i: (b * nq + i, COL_Q // HEAD_W + h)),
            pl.BlockSpec((seq, HEAD_W), lambda b, h, i: (b, COL_K // HEAD_W + h)),
            pl.BlockSpec((seq, HEAD_W), lambda b, h, i: (b, COL_V // HEAD_W + h)),
            vec(ATTN_HEAD_DIM), vec(ATTN_HEAD_DIM), vec(ATTN_HEAD_DIM), vec(ATTN_HEAD_DIM),
            vec(HEAD_W),
        ],
        out_specs=pl.BlockSpec((tq, HEAD_W), lambda b, h, i: (b * nq + i, h)),
        out_shape=jax.ShapeDtypeStruct((t, ATTN_HEADS * HEAD_W), BF16),
        scratch_shapes=[
            pltpu.VMEM((seq // T_ATTN, VT_ROWS, T_ATTN), BF16),
            pltpu.VMEM((T_ATTN, T_ATTN), F32),
            pltpu.VMEM((T_ATTN, T_ATTN), F32),
            pltpu.VMEM((4, T_ATTN, HEAD_W), BF16),
            pltpu.VMEM((2, T_ATTN, T_ATTN), F32),
            pltpu.VMEM((2, T_ATTN, T_ATTN), F32),
            pltpu.VMEM((2, 1, T_ATTN), F32),
            pltpu.VMEM((2, 1, T_ATTN), F32),
            pltpu.VMEM((4, 1, T_ATTN), F32),
            pltpu.VMEM((4, VT_ROWS, T_ATTN), F32),
        ],
        compiler_params=pltpu.CompilerParams(
            dimension_semantics=("parallel", "parallel", "arbitrary"),
            vmem_limit_bytes=VMEM_LIMIT),
        name="diff_attn",
    )(slopes, proj, proj, proj, lq1, lk1, lq2, lk2, g_subln)


def _merge_kernel(y_ref, o_ref, gs_ref, ga_ref, x_ref, bgs_ref, bga_ref, wbs_ref, wba_ref, wo_ref,
                  gn_ref, x1_ref, h2_ref):
    br_ssm = _dot(y_ref[...], wbs_ref[...])
    br_attn = _dot(o_ref[...], wba_ref[...])
    g_ssm = _sigmoid(gs_ref[...].astype(F32) + bgs_ref[...])
    g_attn = _sigmoid(ga_ref[...].astype(F32) + bga_ref[...])
    merged = (g_ssm * br_ssm + g_attn * br_attn).astype(BF16)
    x1 = x_ref[...] + _dot(merged, wo_ref[...])
    x1_ref[...] = x1
    ms = jnp.mean(x1 * x1, axis=-1, keepdims=True)
    h2_ref[...] = (x1 * lax.rsqrt(ms + EPS) * gn_ref[...]).astype(BF16)


def _merge(y_ssm, o_attn, proj, x2, bg_ssm, bg_attn, w_bs, w_ba, w_o, g_mlp):
    t = x2.shape[0]
    rows = lambda n: pl.BlockSpec((TM_MERGE, n), lambda i: (i, 0))
    const = lambda shape: pl.BlockSpec(shape, lambda i: (0, 0))
    return pl.pallas_call(
        _merge_kernel,
        grid=(t // TM_MERGE,),
        in_specs=[
            rows(D_INNER),
            rows(D_MODEL),
            pl.BlockSpec((TM_MERGE, D_MODEL), lambda i: (i, COL_GATE // D_MODEL)),
            pl.BlockSpec((TM_MERGE, D_MODEL), lambda i: (i, COL_GATE // D_MODEL + 1)),
            rows(D_MODEL),
            const((1, D_MODEL)), const((1, D_MODEL)),
            const((D_INNER, D_MODEL)), const((D_MODEL, D_MODEL)), const((D_MODEL, D_MODEL)),
            const((1, D_MODEL)),
        ],
        out_specs=[rows(D_MODEL), rows(D_MODEL)],
        out_shape=[jax.ShapeDtypeStruct((t, D_MODEL), F32), jax.ShapeDtypeStruct((t, D_MODEL), BF16)],
        compiler_params=pltpu.CompilerParams(
            dimension_semantics=("parallel",), vmem_limit_bytes=VMEM_LIMIT),
        name="merge",
    )(y_ssm, o_attn, proj, proj, x2, bg_ssm, bg_attn, w_bs, w_ba, w_o, g_mlp)


def _mlp_kernel(x1_ref, h2_ref, wu_ref, wd_ref, gf_ref, out_ref):
    h2 = h2_ref[...]
    acc = x1_ref[...]
    for c in range(D_FF // FF_CHUNK):
        u = jnp.maximum(_dot(h2, wu_ref[:, c * FF_CHUNK:(c + 1) * FF_CHUNK]), 0.0)
        acc = acc + _dot((u * u).astype(BF16), wd_ref[c * FF_CHUNK:(c + 1) * FF_CHUNK, :])
    ms = jnp.mean(acc * acc, axis=-1, keepdims=True)
    out_ref[...] = acc * lax.rsqrt(ms + EPS) * gf_ref[...]


def _mlp(x1, h2, w_up, w_down, g_final):
    t = x1.shape[0]
    rows = lambda n: pl.BlockSpec((TM_MLP, n), lambda i: (i, 0))
    const = lambda shape: pl.BlockSpec(shape, lambda i: (0, 0), pipeline_mode=pl.Buffered(1))
    return pl.pallas_call(
        _mlp_kernel,
        grid=(t // TM_MLP,),
        in_specs=[rows(D_MODEL), rows(D_MODEL), const((D_MODEL, D_FF)), const((D_FF, D_MODEL)),
                  const((1, D_MODEL))],
        out_specs=rows(D_MODEL),
        out_shape=jax.ShapeDtypeStruct((t, D_MODEL), F32),
        compiler_params=pltpu.CompilerParams(
            dimension_semantics=("parallel",), vmem_limit_bytes=VMEM_LIMIT),
        name="mlp",
    )(x1, h2, w_up, w_down, g_final)


def kernel(x, g_norm_mix, w_in, b_gate, conv_w, conv_b, dt_bias, a_log, d_skip, g_ssm_norm,
           lambda_q1, lambda_k1, lambda_q2, lambda_k2, g_subln, w_br_ssm, w_br_attn, w_out,
           g_norm_mlp, w_up, w_down, g_norm_final):
    bsz, seq, _ = x.shape
    depth = w_in.shape[0]
    x2 = x.reshape(bsz * seq, D_MODEL)

    lane_head = jnp.arange(LANES)[:, None] % SSM_HEADS
    live = jnp.arange(LANES)[:, None] < 3 * SSM_HEADS
    sel = (live & (lane_head == jnp.arange(SSM_HEADS * L_SSD)[None, :] // L_SSD)).astype(BF16)
    sel_wide = (live & (lane_head == jnp.arange(D_INNER)[None, :] // SSM_HEAD_DIM)).astype(BF16)
    head_mask = (jnp.arange(GROUP_W)[None, None, :] // SSM_HEAD_DIM
                 == jnp.arange(HEADS_PER_GROUP)[:, None, None]).astype(BF16)
    slopes = jnp.exp2(-8.0 * jnp.arange(1, ATTN_HEADS + 1, dtype=F32) / ATTN_HEADS)

    for l in range(depth):
        w = w_in[l]
        s_z, s_xbc, s_dt = D_INNER, D_INNER + CONV_DIM, D_INNER + CONV_DIM + SSM_HEADS
        w_slab = jnp.concatenate([w[:, s_z:s_xbc], 0.5 * w[:, :s_z], w[:, s_dt:]], axis=1).astype(BF16)
        rep = LANES // SSM_HEADS
        w_dt = jnp.tile(w[:, s_xbc:s_dt], (1, rep)).astype(BF16)
        dtb = jnp.tile(dt_bias[l], rep).reshape(1, LANES)
        a_row = jnp.tile(-jnp.exp(a_log[l].astype(F32)) * LOG2E, rep).reshape(1, LANES)
        dskip_w = jnp.repeat(d_skip[l], SSM_HEAD_DIM).reshape(1, D_INNER)
        lam_init = 0.8 - 0.6 * math.exp(-0.3 * l)

        proj, dt = _inproj(x2, g_norm_mix[l].reshape(1, D_MODEL), w_slab, w_dt, dtb)
        y_ssm = _ssd(proj, dt, (0.5 * conv_w[l]).reshape(CONV_WIDTH, N_SLABS, 1, LANES),
                     (0.5 * conv_b[l]).reshape(N_SLABS, 1, LANES), a_row, dskip_w,
                     g_ssm_norm[l].reshape(1, D_INNER), sel, sel_wide, head_mask, bsz, seq)
        o_attn = _attention(proj, slopes,
                            lambda_q1[l].reshape(1, -1), lambda_k1[l].reshape(1, -1),
                            lambda_q2[l].reshape(1, -1), lambda_k2[l].reshape(1, -1),
                            g_subln[l].reshape(1, HEAD_W), bsz, seq, lam_init)
        x1, h2 = _merge(y_ssm, o_attn, proj, x2,
                        b_gate[l, :D_MODEL].reshape(1, D_MODEL), b_gate[l, D_MODEL:].reshape(1, D_MODEL),
                        w_br_ssm[l].astype(BF16), w_br_attn[l].astype(BF16), w_out[l].astype(BF16),
                        g_norm_mlp[l].reshape(1, D_MODEL))
        assert l == depth - 1
        x2 = _mlp(x1, h2, w_up[l].astype(BF16), w_down[l].astype(BF16),
                  g_norm_final.reshape(1, D_MODEL))
    return x2.reshape(bsz, seq, D_MODEL)
```

```python
import functools
import math

import jax
import jax.numpy as jnp
from jax import lax
from jax.experimental import pallas as pl
from jax.experimental.pallas import tpu as pltpu

F32 = jnp.float32
BF16 = jnp.bfloat16

D_MODEL = 1024
CHUNK = 64
D_INNER = 2048
SSM_HEAD_DIM = 64
SSM_HEADS = 32
SSM_GROUPS = 8
HEADS_PER_GROUP = 4
D_STATE = 128
CONV_WIDTH = 4
CONV_DIM = 4096
GROUP_W = D_INNER // SSM_GROUPS
ATTN_HEADS = 8
ATTN_HEAD_DIM = 64
HEAD_W = 2 * ATTN_HEAD_DIM
VT_ROWS = HEAD_W + 16
D_FF = 4096
EPS = 1e-5
LOG2E = 1.4426950408889634
NEG_BIG = -1e30

LANES = 128
N_SLABS = CONV_DIM // LANES
HIST = 8

COL_XBC = 0
COL_Z = COL_XBC + CONV_DIM
COL_Q = COL_Z + D_INNER
COL_K = COL_Q + ATTN_HEADS * HEAD_W
COL_V = COL_K + ATTN_HEADS * HEAD_W
COL_GATE = COL_V + ATTN_HEADS * HEAD_W
N_SLAB = COL_GATE + 2 * D_MODEL

TM_PROJ = 1024
TN_PROJ = 2816
L_SSD = 128
T_ATTN = 512
TM_MERGE = 512
TM_MLP = 512
FF_CHUNK = 1024

VMEM_LIMIT = 56 * 1024 * 1024


def _nt_dot(a, b):
    return lax.dot_general(a, b, (((1,), (1,)), ((), ())), preferred_element_type=F32)


def _dot(a, b):
    return jnp.dot(a, b, preferred_element_type=F32)


def _dot_f32(a, b):
    return jnp.dot(a, b, preferred_element_type=F32, precision=lax.Precision.HIGHEST)


def _sigmoid(x):
    return 1.0 / (1.0 + jnp.exp(-x))


def _split3(x):
    p1 = x.astype(BF16)
    r1 = x - p1.astype(F32)
    p2 = r1.astype(BF16)
    p3 = (r1 - p2.astype(F32)).astype(BF16)
    return p1, p2, p3


def _inproj_kernel(x_ref, g_ref, w_ref, wdt_ref, dtb_ref, o_ref, dt_ref, h_ref):
    @pl.when(pl.program_id(1) == 0)
    def _():
        x = x_ref[...]
        ms = jnp.mean(x * x, axis=-1, keepdims=True)
        h = (x * lax.rsqrt(ms + EPS) * g_ref[...]).astype(BF16)
        h_ref[...] = h
        dt_raw = _dot(h, wdt_ref[...]) + dtb_ref[...]
        dt_ref[...] = jnp.maximum(dt_raw, 0.0) + jnp.log(1.0 + jnp.exp(-jnp.abs(dt_raw)))

    o_ref[...] = _dot(h_ref[...], w_ref[...]).astype(BF16)


def _inproj(x2, g, w_slab, w_dt, dt_bias):
    t = x2.shape[0]
    grid = (t // TM_PROJ, N_SLAB // TN_PROJ)
    return pl.pallas_call(
        _inproj_kernel,
        grid=grid,
        in_specs=[
            pl.BlockSpec((TM_PROJ, D_MODEL), lambda i, j: (i, 0)),
            pl.BlockSpec((1, D_MODEL), lambda i, j: (0, 0)),
            pl.BlockSpec((D_MODEL, TN_PROJ), lambda i, j: (0, j)),
            pl.BlockSpec((D_MODEL, LANES), lambda i, j: (0, 0)),
            pl.BlockSpec((1, LANES), lambda i, j: (0, 0)),
        ],
        out_specs=[
            pl.BlockSpec((TM_PROJ, TN_PROJ), lambda i, j: (i, j)),
            pl.BlockSpec((TM_PROJ, LANES), lambda i, j: (i, 0)),
        ],
        out_shape=[
            jax.ShapeDtypeStruct((t, N_SLAB), BF16),
            jax.ShapeDtypeStruct((t, LANES), F32),
        ],
        scratch_shapes=[pltpu.VMEM((TM_PROJ, D_MODEL), BF16)],
        compiler_params=pltpu.CompilerParams(
            dimension_semantics=("parallel", "arbitrary"),
            vmem_limit_bytes=VMEM_LIMIT),
        name="inproj",
    )(x2, g, w_slab, w_dt, dt_bias)


def _silu_of_half(hx):
    return hx + hx * jnp.tanh(hx)


def _ssd_kernel(xbc_ref, z_ref, dt_ref, cw_ref, cb_ref, a_ref, dsk_ref, gn_ref, sel_ref, selw_ref,
                hm_ref, o_ref, xpad_ref, xc_ref, state_ref):
    L = L_SSD
    assert L == D_STATE

    @pl.when(pl.program_id(1) == 0)
    def _():
        state_ref[...] = jnp.zeros_like(state_ref)
        xpad_ref[:, 0:HIST, :] = jnp.zeros((N_SLABS, HIST, LANES), F32)

    for c in range(N_SLABS):
        xpad_ref[c, HIST:HIST + L, :] = xbc_ref[:, c * LANES:(c + 1) * LANES].astype(F32)
    for c in range(N_SLABS):
        taps = [cw_ref[k, c] for k in range(CONV_WIDTH)]
        bias = cb_ref[c]
        for s in [32 * b + j for b in range(L // 32) for j in range(4)]:
            acc = bias
            for k in range(CONV_WIDTH):
                first = HIST + s - (CONV_WIDTH - 1) + k
                acc = acc + xpad_ref[pl.ds(c, 1), pl.ds(first, 8, stride=4), :][0] * taps[k]
            xc_ref[pl.ds(c, 1), pl.ds(s, 8, stride=4), :] = _silu_of_half(acc)[None]
    xpad_ref[:, 0:HIST, :] = xpad_ref[:, L:L + HIST, :]

    dt = dt_ref[...]
    row = lax.broadcasted_iota(jnp.int32, (L, L), 0)
    col = lax.broadcasted_iota(jnp.int32, (L, L), 1)
    tri = row >= col
    acs = _dot_f32(tri.astype(F32), dt * a_ref[...])
    acs_last = acs[L - 1:L, :]
    src_t = (acs - jnp.log2(dt)).T
    to_end_t = jnp.exp2(acs_last - acs + jnp.log2(dt)).T

    a1, a2, a3 = _split3(acs)
    quarter = lax.broadcasted_iota(jnp.int32, (L, LANES), 1) // SSM_HEADS
    pieces = jnp.where(quarter == 0, a1, jnp.where(quarter == 1, a2, jnp.where(quarter == 2, a3, 0)))
    acs_bc = _dot(pieces, sel_ref[...])
    chunk_decay = jnp.exp2(_dot(pieces[L - 8:L, :], selw_ref[...]))[7:8, :]

    for g in range(SSM_GROUPS):
        ch = slice(g * GROUP_W, (g + 1) * GROUP_W)
        xg = jnp.concatenate([xc_ref[2 * g], xc_ref[2 * g + 1]], axis=1)
        bg = xc_ref[D_INNER // LANES + g]
        cg = xc_ref[D_INNER // LANES + SSM_GROUPS + g]
        cb = _nt_dot(cg.astype(BF16), bg.astype(BF16))
        bg_t = bg.T
        st = state_ref[g]
        xs = jnp.concatenate([xg.astype(BF16), st.astype(BF16)], axis=0)
        lhs, rhs, bts = [], [], []
        for r in range(HEADS_PER_GROUP):
            hd = g * HEADS_PER_GROUP + r
            bc = acs_bc[:, hd * L:(hd + 1) * L]
            w = cb * jnp.where(tri, jnp.exp2(bc - src_t[hd:hd + 1, :]), 0.0)
            cs = cg * jnp.exp2(bc)
            lhs += [w.astype(BF16), cs.astype(BF16)]
            rhs.append(xs * hm_ref[r])
            bts.append((bg_t * to_end_t[hd:hd + 1, :]).astype(BF16))
        rhs_all = jnp.concatenate(rhs, axis=0)
        y = xg * dsk_ref[:, ch] + _dot(jnp.concatenate(lhs, axis=1), rhs_all)
        x_heads = jnp.concatenate([rh[0:L, :] for rh in rhs], axis=0)
        state_ref[g] = st * chunk_decay[:, ch] + _dot(jnp.concatenate(bts, axis=1), x_heads)

        y = y * _silu_of_half(z_ref[:, ch].astype(F32))
        ms = jnp.mean(y * y, axis=-1, keepdims=True)
        o_ref[:, ch] = (y * lax.rsqrt(ms + EPS) * gn_ref[:, ch]).astype(BF16)


def _ssd(proj, dt, conv_w, conv_b, a_row, dskip_w, g_norm, sel, sel_wide, head_mask, bsz, seq):
    t = proj.shape[0]
    nc = seq // L_SSD
    const = lambda shape: pl.BlockSpec(shape, lambda b, c: (0,) * len(shape))
    return pl.pallas_call(
        _ssd_kernel,
        grid=(bsz, nc),
        in_specs=[
            pl.BlockSpec((L_SSD, CONV_DIM), lambda b, c: (b * nc + c, COL_XBC // CONV_DIM)),
            pl.BlockSpec((L_SSD, D_INNER), lambda b, c: (b * nc + c, COL_Z // D_INNER)),
            pl.BlockSpec((L_SSD, LANES), lambda b, c: (b * nc + c, 0)),
            const((CONV_WIDTH, N_SLABS, 1, LANES)),
            const((N_SLABS, 1, LANES)),
            const((1, LANES)),
            const((1, D_INNER)),
            const((1, D_INNER)),
            const((LANES, SSM_HEADS * L_SSD)),
            const((LANES, D_INNER)),
            const((HEADS_PER_GROUP, 1, GROUP_W)),
        ],
        out_specs=pl.BlockSpec((L_SSD, D_INNER), lambda b, c: (b * nc + c, 0)),
        out_shape=jax.ShapeDtypeStruct((t, D_INNER), BF16),
        scratch_shapes=[
            pltpu.VMEM((N_SLABS, L_SSD + HIST, LANES), F32),
            pltpu.VMEM((N_SLABS, L_SSD, LANES), F32),
            pltpu.VMEM((SSM_GROUPS, D_STATE, GROUP_W), F32),
        ],
        compiler_params=pltpu.CompilerParams(
            dimension_semantics=("parallel", "arbitrary"),
            vmem_limit_bytes=VMEM_LIMIT),
        name="ssd",
    )(proj, proj, dt, conv_w, conv_b, a_row, dskip_w, g_norm, sel, sel_wide, head_mask)


def _attn_kernel(slopes_ref, q_ref, k_ref, v_ref, lq1_ref, lk1_ref, lq2_ref, lk2_ref, gsub_ref,
                 o_ref, vt_ref, ka_ref, feat_ref, bias_diag_ref, qm_ref, s_a, s_b, cm_a, cm_b, m_ref,
                 acc_ref, *, lam_init):
    T = T_ATTN
    h = pl.program_id(1)
    step = pl.program_id(2)
    slope = slopes_ref[h] * LOG2E

    lane = lax.broadcasted_iota(jnp.int32, (T, HEAD_W), 1)
    data_mask = [((lane >= ATTN_HEAD_DIM * mi) & (lane < ATTN_HEAD_DIM * (mi + 1))).astype(F32).astype(BF16)
                 for mi in range(2)]

    @pl.when(step == 0)
    def _():
        key = lax.broadcasted_iota(jnp.int32, (T, T), 0)
        qry = lax.broadcasted_iota(jnp.int32, (T, T), 1)
        rel = (key - qry).astype(F32)
        visible = (qry // CHUNK) >= (key // CHUNK)
        bias_diag_ref[...] = jnp.where(visible, -slope * jnp.abs(rel), NEG_BIG)
        pos = lax.broadcasted_iota(jnp.int32, (T, HEAD_W), 0).astype(F32)
        sp = tuple(p.astype(F32) for p in _split3(slope * pos))
        one = jnp.ones((T, HEAD_W), F32)
        k_feats = sp + (one, one, one)
        q_feats = (one, one, one) + tuple(-p for p in sp)
        for mi in range(2):
            first = ATTN_HEAD_DIM * (1 - mi)
            fk = jnp.zeros((T, HEAD_W), F32)
            fq = jnp.zeros((T, HEAD_W), F32)
            for n in range(len(k_feats)):
                fk = jnp.where(lane == first + n, k_feats[n], fk)
                fq = jnp.where(lane == first + n, q_feats[n], fq)
            feat_ref[mi] = fk.astype(BF16)
            feat_ref[2 + mi] = fq.astype(BF16)
        for c in range(vt_ref.shape[0]):
            kc = k_ref[c * T:(c + 1) * T, :]
            for mi in range(2):
                ka_ref[mi, c] = kc * data_mask[mi] + feat_ref[mi]
        ones_rows = (lax.broadcasted_iota(jnp.int32, (VT_ROWS - HEAD_W, T), 0) == 0).astype(BF16)
        for c in range(vt_ref.shape[0]):
            vt_ref[c, 0:HEAD_W, :] = v_ref[c * T:(c + 1) * T, :].astype(F32).T.astype(BF16)
            vt_ref[c, HEAD_W:VT_ROWS, :] = ones_rows

    for tile in range(2):
        q = q_ref[tile * T:(tile + 1) * T, :]
        qs = (q.astype(F32) * (ATTN_HEAD_DIM ** -0.5 * LOG2E)).astype(BF16)
        for mi in range(2):
            qm_ref[0, 2 * tile + mi] = qs * data_mask[mi]
            qm_ref[1, 2 * tile + mi] = qs * data_mask[mi] + feat_ref[2 + mi]

    m_ref[...] = jnp.full(m_ref.shape, NEG_BIG, F32)
    acc_ref[...] = jnp.zeros(acc_ref.shape, F32)

    def scores(j, tile, diagonal, s_buf, cm_buf):
        for mi in range(2):
            kj = ka_ref[mi, j]
            if diagonal:
                s = _nt_dot(kj, qm_ref[0, 2 * tile + mi]) + bias_diag_ref[...]
            else:
                s = _nt_dot(kj, qm_ref[1, 2 * tile + mi])
            s_buf[mi] = s
            cm_buf[mi] = jnp.max(s, axis=0, keepdims=True)

    def accumulate(j, tile, s_buf, cm_buf, shift):
        vtj = vt_ref[j]
        for mi in range(2):
            idx = 2 * tile + mi
            m_old = m_ref[idx]
            m_new = jnp.maximum(m_old, cm_buf[mi] + shift)
            alpha = jnp.exp2(m_old - m_new)
            p = jnp.exp2(s_buf[mi] - (m_new - shift))
            acc_ref[idx] = alpha * acc_ref[idx] + _dot(vtj, p.astype(BF16))
            m_ref[idx] = m_new

    def off_shift(j, tile):
        return slope * ((j - (2 * step + tile)) * T).astype(F32)

    d0 = 2 * step
    scores(d0, 0, True, s_a, cm_a)
    scores(d0 + 1, 1, True, s_b, cm_b)
    accumulate(d0, 0, s_a, cm_a, 0.0)

    def body(j, carry):
        j_b, shift_b = carry
        scores(j, 0, False, s_a, cm_a)
        accumulate(j_b, 1, s_b, cm_b, shift_b)
        scores(j, 1, False, s_b, cm_b)
        accumulate(j, 0, s_a, cm_a, off_shift(j, 0))
        return j, off_shift(j, 1)

    j_b, shift_b = lax.fori_loop(0, d0, body, (d0 + 1, jnp.float32(0.0)))
    scores(d0, 1, False, s_a, cm_a)
    accumulate(j_b, 1, s_b, cm_b, shift_b)
    accumulate(d0, 1, s_a, cm_a, off_shift(d0, 1))

    lam = (jnp.exp(jnp.sum(lq1_ref[...] * lk1_ref[...], axis=-1, keepdims=True))
           - jnp.exp(jnp.sum(lq2_ref[...] * lk2_ref[...], axis=-1, keepdims=True)) + lam_init)
    for tile in range(2):
        num = [acc_ref[2 * tile + mi, 0:HEAD_W, :] for mi in range(2)]
        den = [acc_ref[2 * tile + mi, HEAD_W:HEAD_W + 1, :] for mi in range(2)]
        o_t = num[0] * (1.0 / den[0]) - lam * (num[1] * (1.0 / den[1]))
        ms = jnp.mean(o_t * o_t, axis=0, keepdims=True)
        o = (o_t * lax.rsqrt(ms + EPS)).T
        o_ref[tile * T:(tile + 1) * T, :] = (o * gsub_ref[...] * (1.0 - lam_init)).astype(BF16)


def _attention(proj, slopes, lq1, lk1, lq2, lk2, g_subln, bsz, seq, lam_init):
    t = proj.shape[0]
    tq = 2 * T_ATTN
    nq = seq // tq
    vec = lambda n: pl.BlockSpec((1, n), lambda b, h, i: (0, 0))
    return pl.pallas_call(
        functools.partial(_attn_kernel, lam_init=lam_init),
        grid=(bsz, ATTN_HEADS, nq),
        in_specs=[
            pl.BlockSpec(memory_space=pltpu.SMEM),
            pl.BlockSpec((tq, HEAD_W), lambda b, h, i: (b * nq + i, COL_Q // HEAD_W + h)),
            pl.BlockSpec((seq, HEAD_W), lambda b, h, i: (b, COL_K // HEAD_W + h)),
            pl.BlockSpec((seq, HEAD_W), lambda b, h, i: (b, COL_V // HEAD_W + h)),
            vec(ATTN_HEAD_DIM), vec(ATTN_HEAD_DIM), vec(ATTN_HEAD_DIM), vec(ATTN_HEAD_DIM),
            vec(HEAD_W),
        ],
        out_specs=pl.BlockSpec((tq, HEAD_W), lambda b, h, i: (b * nq + i, h)),
        out_shape=jax.ShapeDtypeStruct((t, ATTN_HEADS * HEAD_W), BF16),
        scratch_shapes=[
            pltpu.VMEM((seq // T_ATTN, VT_ROWS, T_ATTN), BF16),
            pltpu.VMEM((2, seq // T_ATTN, T_ATTN, HEAD_W), BF16),
            pltpu.VMEM((4, T_ATTN, HEAD_W), BF16),
            pltpu.VMEM((T_ATTN, T_ATTN), F32),
            pltpu.VMEM((2, 4, T_ATTN, HEAD_W), BF16),
            pltpu.VMEM((2, T_ATTN, T_ATTN), F32),
            pltpu.VMEM((2, T_ATTN, T_ATTN), F32),
            pltpu.VMEM((2, 1, T_ATTN), F32),
            pltpu.VMEM((2, 1, T_ATTN), F32),
            pltpu.VMEM((4, 1, T_ATTN), F32),
            pltpu.VMEM((4, VT_ROWS, T_ATTN), F32),
        ],
        compiler_params=pltpu.CompilerParams(
            dimension_semantics=("parallel", "parallel", "arbitrary"),
            vmem_limit_bytes=VMEM_LIMIT),
        name="diff_attn",
    )(slopes, proj, proj, proj, lq1, lk1, lq2, lk2, g_subln)


def _merge_kernel(y_ref, o_ref, gs_ref, ga_ref, x_ref, bgs_ref, bga_ref, wbs_ref, wba_ref, wo_ref,
                  gn_ref, x1_ref, h2_ref):
    br_ssm = _dot(y_ref[...], wbs_ref[...])
    br_attn = _dot(o_ref[...], wba_ref[...])
    g_ssm = _sigmoid(gs_ref[...].astype(F32) + bgs_ref[...])
    g_attn = _sigmoid(ga_ref[...].astype(F32) + bga_ref[...])
    merged = (g_ssm * br_ssm + g_attn * br_attn).astype(BF16)
    x1 = x_ref[...] + _dot(merged, wo_ref[...])
    x1_ref[...] = x1
    ms = jnp.mean(x1 * x1, axis=-1, keepdims=True)
    h2_ref[...] = (x1 * lax.rsqrt(ms + EPS) * gn_ref[...]).astype(BF16)


def _merge(y_ssm, o_attn, proj, x2, bg_ssm, bg_attn, w_bs, w_ba, w_o, g_mlp):
    t = x2.shape[0]
    rows = lambda n: pl.BlockSpec((TM_MERGE, n), lambda i: (i, 0))
    const = lambda shape: pl.BlockSpec(shape, lambda i: (0, 0))
    return pl.pallas_call(
        _merge_kernel,
        grid=(t // TM_MERGE,),
        in_specs=[
            rows(D_INNER),
            rows(D_MODEL),
            pl.BlockSpec((TM_MERGE, D_MODEL), lambda i: (i, COL_GATE // D_MODEL)),
            pl.BlockSpec((TM_MERGE, D_MODEL), lambda i: (i, COL_GATE // D_MODEL + 1)),
            rows(D_MODEL),
            const((1, D_MODEL)), const((1, D_MODEL)),
            const((D_INNER, D_MODEL)), const((D_MODEL, D_MODEL)), const((D_MODEL, D_MODEL)),
            const((1, D_MODEL)),
        ],
        out_specs=[rows(D_MODEL), rows(D_MODEL)],
        out_shape=[jax.ShapeDtypeStruct((t, D_MODEL), F32), jax.ShapeDtypeStruct((t, D_MODEL), BF16)],
        compiler_params=pltpu.CompilerParams(
            dimension_semantics=("parallel",), vmem_limit_bytes=VMEM_LIMIT),
        name="merge",
    )(y_ssm, o_attn, proj, proj, x2, bg_ssm, bg_attn, w_bs, w_ba, w_o, g_mlp)


def _mlp_kernel(x1_ref, h2_ref, wu_ref, wd_ref, gf_ref, out_ref):
    h2 = h2_ref[...]
    acc = x1_ref[...]
    for c in range(D_FF // FF_CHUNK):
        u = jnp.maximum(_dot(h2, wu_ref[:, c * FF_CHUNK:(c + 1) * FF_CHUNK]), 0.0)
        acc = acc + _dot((u * u).astype(BF16), wd_ref[c * FF_CHUNK:(c + 1) * FF_CHUNK, :])
    ms = jnp.mean(acc * acc, axis=-1, keepdims=True)
    out_ref[...] = acc * lax.rsqrt(ms + EPS) * gf_ref[...]


def _mlp(x1, h2, w_up, w_down, g_final):
    t = x1.shape[0]
    rows = lambda n: pl.BlockSpec((TM_MLP, n), lambda i: (i, 0))
    const = lambda shape: pl.BlockSpec(shape, lambda i: (0, 0), pipeline_mode=pl.Buffered(1))
    return pl.pallas_call(
        _mlp_kernel,
        grid=(t // TM_MLP,),
        in_specs=[rows(D_MODEL), rows(D_MODEL), const((D_MODEL, D_FF)), const((D_FF, D_MODEL)),
                  const((1, D_MODEL))],
        out_specs=rows(D_MODEL),
        out_shape=jax.ShapeDtypeStruct((t, D_MODEL), F32),
        compiler_params=pltpu.CompilerParams(
            dimension_semantics=("parallel",), vmem_limit_bytes=VMEM_LIMIT),
        name="mlp",
    )(x1, h2, w_up, w_down, g_final)


def kernel(x, g_norm_mix, w_in, b_gate, conv_w, conv_b, dt_bias, a_log, d_skip, g_ssm_norm,
           lambda_q1, lambda_k1, lambda_q2, lambda_k2, g_subln, w_br_ssm, w_br_attn, w_out,
           g_norm_mlp, w_up, w_down, g_norm_final):
    bsz, seq, _ = x.shape
    depth = w_in.shape[0]
    x2 = x.reshape(bsz * seq, D_MODEL)

    lane_head = jnp.arange(LANES)[:, None] % SSM_HEADS
    live = jnp.arange(LANES)[:, None] < 3 * SSM_HEADS
    sel = (live & (lane_head == jnp.arange(SSM_HEADS * L_SSD)[None, :] // L_SSD)).astype(BF16)
    sel_wide = (live & (lane_head == jnp.arange(D_INNER)[None, :] // SSM_HEAD_DIM)).astype(BF16)
    head_mask = (jnp.arange(GROUP_W)[None, None, :] // SSM_HEAD_DIM
                 == jnp.arange(HEADS_PER_GROUP)[:, None, None]).astype(BF16)
    slopes = jnp.exp2(-8.0 * jnp.arange(1, ATTN_HEADS + 1, dtype=F32) / ATTN_HEADS)

    for l in range(depth):
        w = w_in[l]
        s_z, s_xbc, s_dt = D_INNER, D_INNER + CONV_DIM, D_INNER + CONV_DIM + SSM_HEADS
        w_slab = jnp.concatenate([w[:, s_z:s_xbc], 0.5 * w[:, :s_z], w[:, s_dt:]], axis=1).astype(BF16)
        rep = LANES // SSM_HEADS
        w_dt = jnp.tile(w[:, s_xbc:s_dt], (1, rep)).astype(BF16)
        dtb = jnp.tile(dt_bias[l], rep).reshape(1, LANES)
        a_row = jnp.tile(-jnp.exp(a_log[l].astype(F32)) * LOG2E, rep).reshape(1, LANES)
        dskip_w = jnp.repeat(d_skip[l], SSM_HEAD_DIM).reshape(1, D_INNER)
        lam_init = 0.8 - 0.6 * math.exp(-0.3 * l)

        proj, dt = _inproj(x2, g_norm_mix[l].reshape(1, D_MODEL), w_slab, w_dt, dtb)
        y_ssm = _ssd(proj, dt, (0.5 * conv_w[l]).reshape(CONV_WIDTH, N_SLABS, 1, LANES),
                     (0.5 * conv_b[l]).reshape(N_SLABS, 1, LANES), a_row, dskip_w,
                     g_ssm_norm[l].reshape(1, D_INNER), sel, sel_wide, head_mask, bsz, seq)
        o_attn = _attention(proj, slopes,
                            lambda_q1[l].reshape(1, -1), lambda_k1[l].reshape(1, -1),
                            lambda_q2[l].reshape(1, -1), lambda_k2[l].reshape(1, -1),
                            g_subln[l].reshape(1, HEAD_W), bsz, seq, lam_init)
        x1, h2 = _merge(y_ssm, o_attn, proj, x2,
                        b_gate[l, :D_MODEL].reshape(1, D_MODEL), b_gate[l, D_MODEL:].reshape(1, D_MODEL),
                        w_br_ssm[l].astype(BF16), w_br_attn[l].astype(BF16), w_out[l].astype(BF16),
                        g_norm_mlp[l].reshape(1, D_MODEL))
        assert l == depth - 1
        x2 = _mlp(x1, h2, w_up[l].astype(BF16), w_down[l].astype(BF16),
                  g_norm_final.reshape(1, D_MODEL))
    return x2.reshape(bsz, seq, D_MODEL)
```

```python
import functools
import math

import jax
import jax.numpy as jnp
from jax import lax
from jax.experimental import pallas as pl
from jax.experimental.pallas import tpu as pltpu

F32 = jnp.float32
BF16 = jnp.bfloat16

D_MODEL = 1024
CHUNK = 64
D_INNER = 2048
SSM_HEAD_DIM = 64
SSM_HEADS = 32
SSM_GROUPS = 8
HEADS_PER_GROUP = 4
D_STATE = 128
CONV_WIDTH = 4
CONV_DIM = 4096
GROUP_W = D_INNER // SSM_GROUPS
ATTN_HEADS = 8
ATTN_HEAD_DIM = 64
HEAD_W = 2 * ATTN_HEAD_DIM
VT_ROWS = HEAD_W + 16
D_FF = 4096
EPS = 1e-5
LOG2E = 1.4426950408889634
NEG_BIG = -1e30

LANES = 128
N_SLABS = CONV_DIM // LANES
HIST = 8

COL_XBC = 0
COL_Z = COL_XBC + CONV_DIM
COL_Q = COL_Z + D_INNER
COL_K = COL_Q + ATTN_HEADS * HEAD_W
COL_V = COL_K + ATTN_HEADS * HEAD_W
COL_GATE = COL_V + ATTN_HEADS * HEAD_W
N_SLAB = COL_GATE + 2 * D_MODEL

TM_PROJ = 1024
TN_PROJ = 2816
L_SSD = 128
T_ATTN = 512
TM_MERGE = 512
TM_MLP = 512
FF_CHUNK = 1024

VMEM_LIMIT = 56 * 1024 * 1024


def _nt_dot(a, b):
    return lax.dot_general(a, b, (((1,), (1,)), ((), ())), preferred_element_type=F32)


def _dot(a, b):
    return jnp.dot(a, b, preferred_element_type=F32)


def _dot_f32(a, b):
    return jnp.dot(a, b, preferred_element_type=F32, precision=lax.Precision.HIGHEST)


def _sigmoid(x):
    return 1.0 / (1.0 + jnp.exp(-x))


def _split3(x):
    p1 = x.astype(BF16)
    r1 = x - p1.astype(F32)
    p2 = r1.astype(BF16)
    p3 = (r1 - p2.astype(F32)).astype(BF16)
    return p1, p2, p3


def _inproj_kernel(x_ref, g_ref, w_ref, wdt_ref, dtb_ref, o_ref, dt_ref, h_ref):
    @pl.when(pl.program_id(1) == 0)
    def _():
        x = x_ref[...]
        ms = jnp.mean(x * x, axis=-1, keepdims=True)
        h = (x * lax.rsqrt(ms + EPS) * g_ref[...]).astype(BF16)
        h_ref[...] = h
        dt_raw = _dot(h, wdt_ref[...]) + dtb_ref[...]
        dt_ref[...] = jnp.maximum(dt_raw, 0.0) + jnp.log(1.0 + jnp.exp(-jnp.abs(dt_raw)))

    o_ref[...] = _dot(h_ref[...], w_ref[...]).astype(BF16)


def _inproj(x2, g, w_slab, w_dt, dt_bias):
    t = x2.shape[0]
    grid = (t // TM_PROJ, N_SLAB // TN_PROJ)
    return pl.pallas_call(
        _inproj_kernel,
        grid=grid,
        in_specs=[
            pl.BlockSpec((TM_PROJ, D_MODEL), lambda i, j: (i, 0)),
            pl.BlockSpec((1, D_MODEL), lambda i, j: (0, 0)),
            pl.BlockSpec((D_MODEL, TN_PROJ), lambda i, j: (0, j)),
            pl.BlockSpec((D_MODEL, LANES), lambda i, j: (0, 0)),
            pl.BlockSpec((1, LANES), lambda i, j: (0, 0)),
        ],
        out_specs=[
            pl.BlockSpec((TM_PROJ, TN_PROJ), lambda i, j: (i, j)),
            pl.BlockSpec((TM_PROJ, LANES), lambda i, j: (i, 0)),
        ],
        out_shape=[
            jax.ShapeDtypeStruct((t, N_SLAB), BF16),
            jax.ShapeDtypeStruct((t, LANES), F32),
        ],
        scratch_shapes=[pltpu.VMEM((TM_PROJ, D_MODEL), BF16)],
        compiler_params=pltpu.CompilerParams(
            dimension_semantics=("parallel", "arbitrary"),
            vmem_limit_bytes=VMEM_LIMIT),
        name="inproj",
    )(x2, g, w_slab, w_dt, dt_bias)


def _silu_of_half(hx):
    return hx + hx * jnp.tanh(hx)


def _ssd_kernel(xbc_ref, z_ref, dt_ref, cw_ref, cb_ref, a_ref, dsk_ref, gn_ref, sel_ref, selw_ref,
                hm_ref, o_ref, xpad_ref, xc_ref, state_ref):
    L = L_SSD
    assert L == D_STATE

    @pl.when(pl.program_id(1) == 0)
    def _():
        state_ref[...] = jnp.zeros_like(state_ref)
        xpad_ref[:, 0:HIST, :] = jnp.zeros((N_SLABS, HIST, LANES), F32)

    for c in range(N_SLABS):
        xpad_ref[c, HIST:HIST + L, :] = xbc_ref[:, c * LANES:(c + 1) * LANES].astype(F32)
    for c in range(N_SLABS):
        taps = [cw_ref[k, c] for k in range(CONV_WIDTH)]
        bias = cb_ref[c]
        for s in [32 * b + j for b in range(L // 32) for j in range(4)]:
            acc = bias
            for k in range(CONV_WIDTH):
                first = HIST + s - (CONV_WIDTH - 1) + k
                acc = acc + xpad_ref[pl.ds(c, 1), pl.ds(first, 8, stride=4), :][0] * taps[k]
            xc_ref[pl.ds(c, 1), pl.ds(s, 8, stride=4), :] = _silu_of_half(acc)[None]
    xpad_ref[:, 0:HIST, :] = xpad_ref[:, L:L + HIST, :]

    dt = dt_ref[...]
    row = lax.broadcasted_iota(jnp.int32, (L, L), 0)
    col = lax.broadcasted_iota(jnp.int32, (L, L), 1)
    tri = row >= col
    acs = _dot_f32(tri.astype(F32), dt * a_ref[...])
    acs_last = acs[L - 1:L, :]
    src_t = (acs - jnp.log2(dt)).T
    to_end_t = jnp.exp2(acs_last - acs + jnp.log2(dt)).T

    a1, a2, a3 = _split3(acs)
    quarter = lax.broadcasted_iota(jnp.int32, (L, LANES), 1) // SSM_HEADS
    pieces = jnp.where(quarter == 0, a1, jnp.where(quarter == 1, a2, jnp.where(quarter == 2, a3, 0)))
    chunk_decay = jnp.exp2(_dot(pieces[L - 8:L, :], selw_ref[...]))[7:8, :]

    for g in range(SSM_GROUPS):
        ch = slice(g * GROUP_W, (g + 1) * GROUP_W)
        xg = jnp.concatenate([xc_ref[2 * g], xc_ref[2 * g + 1]], axis=1)
        bg = xc_ref[D_INNER // LANES + g]
        cg = xc_ref[D_INNER // LANES + SSM_GROUPS + g]
        cb = _nt_dot(cg.astype(BF16), bg.astype(BF16))
        gw = HEADS_PER_GROUP * L
        acs_bc = _dot(pieces, sel_ref[:, g * gw:(g + 1) * gw])
        bg_t = bg.T
        st = state_ref[g]
        xs = jnp.concatenate([xg.astype(BF16), st.astype(BF16)], axis=0)
        lhs, rhs, bts = [], [], []
        for r in range(HEADS_PER_GROUP):
            hd = g * HEADS_PER_GROUP + r
            bc = acs_bc[:, r * L:(r + 1) * L]
            w = cb * jnp.where(tri, jnp.exp2(bc - src_t[hd:hd + 1, :]), 0.0)
            cs = cg * jnp.exp2(bc)
            lhs += [w.astype(BF16), cs.astype(BF16)]
            rhs.append(xs * hm_ref[r])
            bts.append((bg_t * to_end_t[hd:hd + 1, :]).astype(BF16))
        rhs_all = jnp.concatenate(rhs, axis=0)
        y = xg * dsk_ref[:, ch] + _dot(jnp.concatenate(lhs, axis=1), rhs_all)
        x_heads = jnp.concatenate([rh[0:L, :] for rh in rhs], axis=0)
        state_ref[g] = st * chunk_decay[:, ch] + _dot(jnp.concatenate(bts, axis=1), x_heads)

        y = y * _silu_of_half(z_ref[:, ch].astype(F32))
        ms = jnp.mean(y * y, axis=-1, keepdims=True)
        o_ref[:, ch] = (y * lax.rsqrt(ms + EPS) * gn_ref[:, ch]).astype(BF16)


def _ssd(proj, dt, conv_w, conv_b, a_row, dskip_w, g_norm, sel, sel_wide, head_mask, bsz, seq):
    t = proj.shape[0]
    nc = seq // L_SSD
    const = lambda shape: pl.BlockSpec(shape, lambda b, c: (0,) * len(shape))
    return pl.pallas_call(
        _ssd_kernel,
        grid=(bsz, nc),
        in_specs=[
            pl.BlockSpec((L_SSD, CONV_DIM), lambda b, c: (b * nc + c, COL_XBC // CONV_DIM)),
            pl.BlockSpec((L_SSD, D_INNER), lambda b, c: (b * nc + c, COL_Z // D_INNER)),
            pl.BlockSpec((L_SSD, LANES), lambda b, c: (b * nc + c, 0)),
            const((CONV_WIDTH, N_SLABS, 1, LANES)),
            const((N_SLABS, 1, LANES)),
            const((1, LANES)),
            const((1, D_INNER)),
            const((1, D_INNER)),
            const((LANES, SSM_HEADS * L_SSD)),
            const((LANES, D_INNER)),
            const((HEADS_PER_GROUP, 1, GROUP_W)),
        ],
        out_specs=pl.BlockSpec((L_SSD, D_INNER), lambda b, c: (b * nc + c, 0)),
        out_shape=jax.ShapeDtypeStruct((t, D_INNER), BF16),
        scratch_shapes=[
            pltpu.VMEM((N_SLABS, L_SSD + HIST, LANES), F32),
            pltpu.VMEM((N_SLABS, L_SSD, LANES), F32),
            pltpu.VMEM((SSM_GROUPS, D_STATE, GROUP_W), F32),
        ],
        compiler_params=pltpu.CompilerParams(
            dimension_semantics=("parallel", "arbitrary"),
            vmem_limit_bytes=VMEM_LIMIT),
        name="ssd",
    )(proj, proj, dt, conv_w, conv_b, a_row, dskip_w, g_norm, sel, sel_wide, head_mask)


def _attn_kernel(slopes_ref, q_ref, k_ref, v_ref, lq1_ref, lk1_ref, lq2_ref, lk2_ref, gsub_ref,
                 o_ref, vt_ref, ka_ref, feat_ref, bias_diag_ref, qm_ref, s_a, s_b, cm_a, cm_b, m_ref,
                 acc_ref, *, lam_init):
    T = T_ATTN
    h = pl.program_id(1)
    step = pl.program_id(2)
    slope = slopes_ref[h] * LOG2E

    lane = lax.broadcasted_iota(jnp.int32, (T, HEAD_W), 1)
    data_mask = [((lane >= ATTN_HEAD_DIM * mi) & (lane < ATTN_HEAD_DIM * (mi + 1))).astype(F32).astype(BF16)
                 for mi in range(2)]

    @pl.when(step == 0)
    def _():
        key = lax.broadcasted_iota(jnp.int32, (T, T), 0)
        qry = lax.broadcasted_iota(jnp.int32, (T, T), 1)
        rel = (key - qry).astype(F32)
        visible = (qry // CHUNK) >= (key // CHUNK)
        bias_diag_ref[...] = jnp.where(visible, -slope * jnp.abs(rel), NEG_BIG)
        pos = lax.broadcasted_iota(jnp.int32, (T, HEAD_W), 0).astype(F32)
        sp = tuple(p.astype(F32) for p in _split3(slope * pos))
        one = jnp.ones((T, HEAD_W), F32)
        k_feats = sp + (one, one, one)
        q_feats = (one, one, one) + tuple(-p for p in sp)
        for mi in range(2):
            first = ATTN_HEAD_DIM * (1 - mi)
            fk = jnp.zeros((T, HEAD_W), F32)
            fq = jnp.zeros((T, HEAD_W), F32)
            for n in range(len(k_feats)):
                fk = jnp.where(lane == first + n, k_feats[n], fk)
                fq = jnp.where(lane == first + n, q_feats[n], fq)
            feat_ref[mi] = fk.astype(BF16)
            feat_ref[2 + mi] = fq.astype(BF16)
        for c in range(vt_ref.shape[0]):
            kc = k_ref[c * T:(c + 1) * T, :]
            for mi in range(2):
                ka_ref[mi, c] = kc * data_mask[mi] + feat_ref[mi]
        ones_rows = (lax.broadcasted_iota(jnp.int32, (VT_ROWS - HEAD_W, T), 0) == 0).astype(BF16)
        for c in range(vt_ref.shape[0]):
            vt_ref[c, 0:HEAD_W, :] = v_ref[c * T:(c + 1) * T, :].astype(F32).T.astype(BF16)
            vt_ref[c, HEAD_W:VT_ROWS, :] = ones_rows

    for tile in range(2):
        q = q_ref[tile * T:(tile + 1) * T, :]
        qs = (q.astype(F32) * (ATTN_HEAD_DIM ** -0.5 * LOG2E)).astype(BF16)
        for mi in range(2):
            qm_ref[0, 2 * tile + mi] = qs * data_mask[mi]
            qm_ref[1, 2 * tile + mi] = qs * data_mask[mi] + feat_ref[2 + mi]

    m_ref[...] = jnp.full(m_ref.shape, NEG_BIG, F32)
    acc_ref[...] = jnp.zeros(acc_ref.shape, F32)

    def scores(j, tile, diagonal, s_buf, cm_buf):
        for mi in range(2):
            kj = ka_ref[mi, j]
            if diagonal:
                s = _nt_dot(kj, qm_ref[0, 2 * tile + mi]) + bias_diag_ref[...]
            else:
                s = _nt_dot(kj, qm_ref[1, 2 * tile + mi])
            s_buf[mi] = s
            cm_buf[mi] = jnp.max(s, axis=0, keepdims=True)

    def accumulate(j, tile, s_buf, cm_buf, shift):
        vtj = vt_ref[j]
        for mi in range(2):
            idx = 2 * tile + mi
            m_old = m_ref[idx]
            m_new = jnp.maximum(m_old, cm_buf[mi] + shift)
            alpha = jnp.exp2(m_old - m_new)
            p = jnp.exp2(s_buf[mi] - (m_new - shift))
            acc_ref[idx] = alpha * acc_ref[idx] + _dot(vtj, p.astype(BF16))
            m_ref[idx] = m_new

    def off_shift(j, tile):
        return slope * ((j - (2 * step + tile)) * T).astype(F32)

    d0 = 2 * step
    scores(d0, 0, True, s_a, cm_a)
    scores(d0 + 1, 1, True, s_b, cm_b)
    accumulate(d0, 0, s_a, cm_a, 0.0)

    def body(t, carry):
        j_b, shift_b = carry
        for j in (2 * t, 2 * t + 1):
            scores(j, 0, False, s_a, cm_a)
            accumulate(j_b, 1, s_b, cm_b, shift_b)
            scores(j, 1, False, s_b, cm_b)
            accumulate(j, 0, s_a, cm_a, off_shift(j, 0))
            j_b, shift_b = j, off_shift(j, 1)
        return j_b, shift_b

    j_b, shift_b = lax.fori_loop(0, step, body, (d0 + 1, jnp.float32(0.0)))
    scores(d0, 1, False, s_a, cm_a)
    accumulate(j_b, 1, s_b, cm_b, shift_b)
    accumulate(d0, 1, s_a, cm_a, off_shift(d0, 1))

    lam = (jnp.exp(jnp.sum(lq1_ref[...] * lk1_ref[...], axis=-1, keepdims=True))
           - jnp.exp(jnp.sum(lq2_ref[...] * lk2_ref[...], axis=-1, keepdims=True)) + lam_init)
    for tile in range(2):
        num = [acc_ref[2 * tile + mi, 0:HEAD_W, :] for mi in range(2)]
        den = [acc_ref[2 * tile + mi, HEAD_W:HEAD_W + 1, :] for mi in range(2)]
        o_t = num[0] * (1.0 / den[0]) - lam * (num[1] * (1.0 / den[1]))
        ms = jnp.mean(o_t * o_t, axis=0, keepdims=True)
        o = (o_t * lax.rsqrt(ms + EPS)).T
        o_ref[tile * T:(tile + 1) * T, :] = (o * gsub_ref[...] * (1.0 - lam_init)).astype(BF16)


def _attention(proj, slopes, lq1, lk1, lq2, lk2, g_subln, bsz, seq, lam_init):
    t = proj.shape[0]
    tq = 2 * T_ATTN
    nq = seq // tq
    vec = lambda n: pl.BlockSpec((1, n), lambda b, h, i: (0, 0))
    return pl.pallas_call(
        functools.partial(_attn_kernel, lam_init=lam_init),
        grid=(bsz, ATTN_HEADS, nq),
        in_specs=[
            pl.BlockSpec(memory_space=pltpu.SMEM),
            pl.BlockSpec((tq, HEAD_W), lambda b, h, i: (b * nq + i, COL_Q // HEAD_W + h)),
            pl.BlockSpec((seq, HEAD_W), lambda b, h, i: (b, COL_K // HEAD_W + h)),
            pl.BlockSpec((seq, HEAD_W), lambda b, h, i: (b, COL_V // HEAD_W + h)),
            vec(ATTN_HEAD_DIM), vec(ATTN_HEAD_DIM), vec(ATTN_HEAD_DIM), vec(ATTN_HEAD_DIM),
            vec(HEAD_W),
        ],
        out_specs=pl.BlockSpec((tq, HEAD_W), lambda b, h, i: (b * nq + i, h)),
        out_shape=jax.ShapeDtypeStruct((t, ATTN_HEADS * HEAD_W), BF16),
        scratch_shapes=[
            pltpu.VMEM((seq // T_ATTN, VT_ROWS, T_ATTN), BF16),
            pltpu.VMEM((2, seq // T_ATTN, T_ATTN, HEAD_W), BF16),
            pltpu.VMEM((4, T_ATTN, HEAD_W), BF16),
            pltpu.VMEM((T_ATTN, T_ATTN), F32),
            pltpu.VMEM((2, 4, T_ATTN, HEAD_W), BF16),
            pltpu.VMEM((2, T_ATTN, T_ATTN), F32),
            pltpu.VMEM((2, T_ATTN, T_ATTN), F32),
            pltpu.VMEM((2, 1, T_ATTN), F32),
            pltpu.VMEM((2, 1, T_ATTN), F32),
            pltpu.VMEM((4, 1, T_ATTN), F32),
            pltpu.VMEM((4, VT_ROWS, T_ATTN), F32),
        ],
        compiler_params=pltpu.CompilerParams(
            dimension_semantics=("parallel", "parallel", "arbitrary"),
            vmem_limit_bytes=VMEM_LIMIT),
        name="diff_attn",
    )(slopes, proj, proj, proj, lq1, lk1, lq2, lk2, g_subln)


def _merge_kernel(y_ref, o_ref, gs_ref, ga_ref, x_ref, bgs_ref, bga_ref, wbs_ref, wba_ref, wo_ref,
                  gn_ref, x1_ref, h2_ref):
    br_ssm = _dot(y_ref[...], wbs_ref[...])
    br_attn = _dot(o_ref[...], wba_ref[...])
    g_ssm = _sigmoid(gs_ref[...].astype(F32) + bgs_ref[...])
    g_attn = _sigmoid(ga_ref[...].astype(F32) + bga_ref[...])
    merged = (g_ssm * br_ssm + g_attn * br_attn).astype(BF16)
    x1 = x_ref[...] + _dot(merged, wo_ref[...])
    x1_ref[...] = x1
    ms = jnp.mean(x1 * x1, axis=-1, keepdims=True)
    h2_ref[...] = (x1 * lax.rsqrt(ms + EPS) * gn_ref[...]).astype(BF16)


def _merge(y_ssm, o_attn, proj, x2, bg_ssm, bg_attn, w_bs, w_ba, w_o, g_mlp):
    t = x2.shape[0]
    rows = lambda n: pl.BlockSpec((TM_MERGE, n), lambda i: (i, 0))
    const = lambda shape: pl.BlockSpec(shape, lambda i: (0, 0))
    return pl.pallas_call(
        _merge_kernel,
        grid=(t // TM_MERGE,),
        in_specs=[
            rows(D_INNER),
            rows(D_MODEL),
            pl.BlockSpec((TM_MERGE, D_MODEL), lambda i: (i, COL_GATE // D_MODEL)),
            pl.BlockSpec((TM_MERGE, D_MODEL), lambda i: (i, COL_GATE // D_MODEL + 1)),
            rows(D_MODEL),
            const((1, D_MODEL)), const((1, D_MODEL)),
            const((D_INNER, D_MODEL)), const((D_MODEL, D_MODEL)), const((D_MODEL, D_MODEL)),
            const((1, D_MODEL)),
        ],
        out_specs=[rows(D_MODEL), rows(D_MODEL)],
        out_shape=[jax.ShapeDtypeStruct((t, D_MODEL), F32), jax.ShapeDtypeStruct((t, D_MODEL), BF16)],
        compiler_params=pltpu.CompilerParams(
            dimension_semantics=("parallel",), vmem_limit_bytes=VMEM_LIMIT),
        name="merge",
    )(y_ssm, o_attn, proj, proj, x2, bg_ssm, bg_attn, w_bs, w_ba, w_o, g_mlp)


def _mlp_kernel(x1_ref, h2_ref, wu_ref, wd_ref, gf_ref, out_ref):
    h2 = h2_ref[...]
    acc = x1_ref[...]
    for c in range(D_FF // FF_CHUNK):
        u = jnp.maximum(_dot(h2, wu_ref[:, c * FF_CHUNK:(c + 1) * FF_CHUNK]), 0.0)
        acc = acc + _dot((u * u).astype(BF16), wd_ref[c * FF_CHUNK:(c + 1) * FF_CHUNK, :])
    ms = jnp.mean(acc * acc, axis=-1, keepdims=True)
    out_ref[...] = acc * lax.rsqrt(ms + EPS) * gf_ref[...]


def _mlp(x1, h2, w_up, w_down, g_final):
    t = x1.shape[0]
    rows = lambda n: pl.BlockSpec((TM_MLP, n), lambda i: (i, 0))
    const = lambda shape: pl.BlockSpec(shape, lambda i: (0, 0), pipeline_mode=pl.Buffered(1))
    return pl.pallas_call(
        _mlp_kernel,
        grid=(t // TM_MLP,),
        in_specs=[rows(D_MODEL), rows(D_MODEL), const((D_MODEL, D_FF)), const((D_FF, D_MODEL)),
                  const((1, D_MODEL))],
        out_specs=rows(D_MODEL),
        out_shape=jax.ShapeDtypeStruct((t, D_MODEL), F32),
        compiler_params=pltpu.CompilerParams(
            dimension_semantics=("parallel",), vmem_limit_bytes=VMEM_LIMIT),
        name="mlp",
    )(x1, h2, w_up, w_down, g_final)


def kernel(x, g_norm_mix, w_in, b_gate, conv_w, conv_b, dt_bias, a_log, d_skip, g_ssm_norm,
           lambda_q1, lambda_k1, lambda_q2, lambda_k2, g_subln, w_br_ssm, w_br_attn, w_out,
           g_norm_mlp, w_up, w_down, g_norm_final):
    bsz, seq, _ = x.shape
    depth = w_in.shape[0]
    x2 = x.reshape(bsz * seq, D_MODEL)

    lane_head = jnp.arange(LANES)[:, None] % SSM_HEADS
    live = jnp.arange(LANES)[:, None] < 3 * SSM_HEADS
    sel = (live & (lane_head == jnp.arange(SSM_HEADS * L_SSD)[None, :] // L_SSD)).astype(BF16)
    sel_wide = (live & (lane_head == jnp.arange(D_INNER)[None, :] // SSM_HEAD_DIM)).astype(BF16)
    head_mask = (jnp.arange(GROUP_W)[None, None, :] // SSM_HEAD_DIM
                 == jnp.arange(HEADS_PER_GROUP)[:, None, None]).astype(BF16)
    slopes = jnp.exp2(-8.0 * jnp.arange(1, ATTN_HEADS + 1, dtype=F32) / ATTN_HEADS)

    for l in range(depth):
        w = w_in[l]
        s_z, s_xbc, s_dt = D_INNER, D_INNER + CONV_DIM, D_INNER + CONV_DIM + SSM_HEADS
        w_slab = jnp.concatenate([w[:, s_z:s_xbc], 0.5 * w[:, :s_z], w[:, s_dt:]], axis=1).astype(BF16)
        rep = LANES // SSM_HEADS
        w_dt = jnp.tile(w[:, s_xbc:s_dt], (1, rep)).astype(BF16)
        dtb = jnp.tile(dt_bias[l], rep).reshape(1, LANES)
        a_row = jnp.tile(-jnp.exp(a_log[l].astype(F32)) * LOG2E, rep).reshape(1, LANES)
        dskip_w = jnp.repeat(d_skip[l], SSM_HEAD_DIM).reshape(1, D_INNER)
        lam_init = 0.8 - 0.6 * math.exp(-0.3 * l)

        proj, dt = _inproj(x2, g_norm_mix[l].reshape(1, D_MODEL), w_slab, w_dt, dtb)
        y_ssm = _ssd(proj, dt, (0.5 * conv_w[l]).reshape(CONV_WIDTH, N_SLABS, 1, LANES),
                     (0.5 * conv_b[l]).reshape(N_SLABS, 1, LANES), a_row, dskip_w,
                     g_ssm_norm[l].reshape(1, D_INNER), sel, sel_wide, head_mask, bsz, seq)
        o_attn = _attention(proj, slopes,
                            lambda_q1[l].reshape(1, -1), lambda_k1[l].reshape(1, -1),
                            lambda_q2[l].reshape(1, -1), lambda_k2[l].reshape(1, -1),
                            g_subln[l].reshape(1, HEAD_W), bsz, seq, lam_init)
        x1, h2 = _merge(y_ssm, o_attn, proj, x2,
                        b_gate[l, :D_MODEL].reshape(1, D_MODEL), b_gate[l, D_MODEL:].reshape(1, D_MODEL),
                        w_br_ssm[l].astype(BF16), w_br_attn[l].astype(BF16), w_out[l].astype(BF16),
                        g_norm_mlp[l].reshape(1, D_MODEL))
        assert l == depth - 1
        x2 = _mlp(x1, h2, w_up[l].astype(BF16), w_down[l].astype(BF16),
                  g_norm_final.reshape(1, D_MODEL))
    return x2.reshape(bsz, seq, D_MODEL)
```

```python
import functools
import math

import jax
import jax.numpy as jnp
from jax import lax
from jax.experimental import pallas as pl
from jax.experimental.pallas import tpu as pltpu

F32 = jnp.float32
BF16 = jnp.bfloat16

D_MODEL = 1024
CHUNK = 64
D_INNER = 2048
SSM_HEAD_DIM = 64
SSM_HEADS = 32
SSM_GROUPS = 8
HEADS_PER_GROUP = 4
D_STATE = 128
CONV_WIDTH = 4
CONV_DIM = 4096
GROUP_W = D_INNER // SSM_GROUPS
ATTN_HEADS = 8
ATTN_HEAD_DIM = 64
HEAD_W = 2 * ATTN_HEAD_DIM
VT_ROWS = HEAD_W + 16
D_FF = 4096
EPS = 1e-5
LOG2E = 1.4426950408889634
NEG_BIG = -1e30
UNDERFLOW_LOG2 = 152.0
NORM_SLACK = 1.02

LANES = 128
N_SLABS = CONV_DIM // LANES
HIST = 8

COL_XBC = 0
COL_Z = COL_XBC + CONV_DIM
COL_Q = COL_Z + D_INNER
COL_K = COL_Q + ATTN_HEADS * HEAD_W
COL_V = COL_K + ATTN_HEADS * HEAD_W
COL_GATE = COL_V + ATTN_HEADS * HEAD_W
N_SLAB = COL_GATE + 2 * D_MODEL

TM_PROJ = 1024
TN_PROJ = 2816
L_SSD = 128
T_ATTN = 512
TM_MERGE = 512
TM_MLP = 512
FF_CHUNK = 1024

VMEM_LIMIT = 56 * 1024 * 1024


def _nt_dot(a, b):
    return lax.dot_general(a, b, (((1,), (1,)), ((), ())), preferred_element_type=F32)


def _dot(a, b):
    return jnp.dot(a, b, preferred_element_type=F32)


def _dot_f32(a, b):
    return jnp.dot(a, b, preferred_element_type=F32, precision=lax.Precision.HIGHEST)


def _sigmoid(x):
    return 1.0 / (1.0 + jnp.exp(-x))


def _split3(x):
    p1 = x.astype(BF16)
    r1 = x - p1.astype(F32)
    p2 = r1.astype(BF16)
    p3 = (r1 - p2.astype(F32)).astype(BF16)
    return p1, p2, p3


def _inproj_kernel(x_ref, g_ref, w_ref, wdt_ref, dtb_ref, o_ref, dt_ref, h_ref):
    @pl.when(pl.program_id(1) == 0)
    def _():
        x = x_ref[...]
        ms = jnp.mean(x * x, axis=-1, keepdims=True)
        h = (x * lax.rsqrt(ms + EPS) * g_ref[...]).astype(BF16)
        h_ref[...] = h
        dt_raw = _dot(h, wdt_ref[...]) + dtb_ref[...]
        dt_ref[...] = jnp.maximum(dt_raw, 0.0) + jnp.log(1.0 + jnp.exp(-jnp.abs(dt_raw)))

    o_ref[...] = _dot(h_ref[...], w_ref[...]).astype(BF16)


def _inproj(x2, g, w_slab, w_dt, dt_bias):
    t = x2.shape[0]
    grid = (t // TM_PROJ, N_SLAB // TN_PROJ)
    return pl.pallas_call(
        _inproj_kernel,
        grid=grid,
        in_specs=[
            pl.BlockSpec((TM_PROJ, D_MODEL), lambda i, j: (i, 0)),
            pl.BlockSpec((1, D_MODEL), lambda i, j: (0, 0)),
            pl.BlockSpec((D_MODEL, TN_PROJ), lambda i, j: (0, j)),
            pl.BlockSpec((D_MODEL, LANES), lambda i, j: (0, 0)),
            pl.BlockSpec((1, LANES), lambda i, j: (0, 0)),
        ],
        out_specs=[
            pl.BlockSpec((TM_PROJ, TN_PROJ), lambda i, j: (i, j)),
            pl.BlockSpec((TM_PROJ, LANES), lambda i, j: (i, 0)),
        ],
        out_shape=[
            jax.ShapeDtypeStruct((t, N_SLAB), BF16),
            jax.ShapeDtypeStruct((t, LANES), F32),
        ],
        scratch_shapes=[pltpu.VMEM((TM_PROJ, D_MODEL), BF16)],
        compiler_params=pltpu.CompilerParams(
            dimension_semantics=("parallel", "arbitrary"),
            vmem_limit_bytes=VMEM_LIMIT),
        name="inproj",
    )(x2, g, w_slab, w_dt, dt_bias)


def _silu_of_half(hx):
    return hx + hx * jnp.tanh(hx)


def _ssd_kernel(xbc_ref, z_ref, dt_ref, cw_ref, cb_ref, a_ref, dsk_ref, gn_ref, sel_ref, selw_ref,
                hm_ref, o_ref, xpad_ref, xc_ref, state_ref):
    L = L_SSD
    assert L == D_STATE

    @pl.when(pl.program_id(1) == 0)
    def _():
        state_ref[...] = jnp.zeros_like(state_ref)
        xpad_ref[:, 0:HIST, :] = jnp.zeros((N_SLABS, HIST, LANES), F32)

    for c in range(N_SLABS):
        xpad_ref[c, HIST:HIST + L, :] = xbc_ref[:, c * LANES:(c + 1) * LANES].astype(F32)
    for c in range(N_SLABS):
        taps = [cw_ref[k, c] for k in range(CONV_WIDTH)]
        bias = cb_ref[c]
        for s in [32 * b + j for b in range(L // 32) for j in range(4)]:
            acc = bias
            for k in range(CONV_WIDTH):
                first = HIST + s - (CONV_WIDTH - 1) + k
                acc = acc + xpad_ref[pl.ds(c, 1), pl.ds(first, 8, stride=4), :][0] * taps[k]
            xc_ref[pl.ds(c, 1), pl.ds(s, 8, stride=4), :] = _silu_of_half(acc)[None]
    xpad_ref[:, 0:HIST, :] = xpad_ref[:, L:L + HIST, :]

    dt = dt_ref[...]
    row = lax.broadcasted_iota(jnp.int32, (L, L), 0)
    col = lax.broadcasted_iota(jnp.int32, (L, L), 1)
    tri = row >= col
    acs = _dot_f32(tri.astype(F32), dt * a_ref[...])
    acs_last = acs[L - 1:L, :]
    src_t = (acs - jnp.log2(dt)).T
    to_end_t = jnp.exp2(acs_last - acs + jnp.log2(dt)).T

    a1, a2, a3 = _split3(acs)
    quarter = lax.broadcasted_iota(jnp.int32, (L, LANES), 1) // SSM_HEADS
    pieces = jnp.where(quarter == 0, a1, jnp.where(quarter == 1, a2, jnp.where(quarter == 2, a3, 0)))
    chunk_decay = jnp.exp2(_dot(pieces[L - 8:L, :], selw_ref[...]))[7:8, :]

    for g in range(SSM_GROUPS):
        ch = slice(g * GROUP_W, (g + 1) * GROUP_W)
        xg = jnp.concatenate([xc_ref[2 * g], xc_ref[2 * g + 1]], axis=1)
        bg = xc_ref[D_INNER // LANES + g]
        cg = xc_ref[D_INNER // LANES + SSM_GROUPS + g]
        cb = _nt_dot(cg.astype(BF16), bg.astype(BF16))
        gw = HEADS_PER_GROUP * L
        acs_bc = _dot(pieces, sel_ref[:, g * gw:(g + 1) * gw])
        bg_t = bg.T
        st = state_ref[g]
        xs = jnp.concatenate([xg.astype(BF16), st.astype(BF16)], axis=0)
        lhs, rhs, bts = [], [], []
        for r in range(HEADS_PER_GROUP):
            hd = g * HEADS_PER_GROUP + r
            bc = acs_bc[:, r * L:(r + 1) * L]
            w = cb * jnp.where(tri, jnp.exp2(bc - src_t[hd:hd + 1, :]), 0.0)
            cs = cg * jnp.exp2(bc)
            lhs += [w.astype(BF16), cs.astype(BF16)]
            rhs.append(xs * hm_ref[r])
            bts.append((bg_t * to_end_t[hd:hd + 1, :]).astype(BF16))
        rhs_all = jnp.concatenate(rhs, axis=0)
        y = xg * dsk_ref[:, ch] + _dot(jnp.concatenate(lhs, axis=1), rhs_all)
        x_heads = jnp.concatenate([rh[0:L, :] for rh in rhs], axis=0)
        state_ref[g] = st * chunk_decay[:, ch] + _dot(jnp.concatenate(bts, axis=1), x_heads)

        y = y * _silu_of_half(z_ref[:, ch].astype(F32))
        ms = jnp.mean(y * y, axis=-1, keepdims=True)
        o_ref[:, ch] = (y * lax.rsqrt(ms + EPS) * gn_ref[:, ch]).astype(BF16)


def _ssd(proj, dt, conv_w, conv_b, a_row, dskip_w, g_norm, sel, sel_wide, head_mask, bsz, seq):
    t = proj.shape[0]
    nc = seq // L_SSD
    const = lambda shape: pl.BlockSpec(shape, lambda b, c: (0,) * len(shape))
    return pl.pallas_call(
        _ssd_kernel,
        grid=(bsz, nc),
        in_specs=[
            pl.BlockSpec((L_SSD, CONV_DIM), lambda b, c: (b * nc + c, COL_XBC // CONV_DIM)),
            pl.BlockSpec((L_SSD, D_INNER), lambda b, c: (b * nc + c, COL_Z // D_INNER)),
            pl.BlockSpec((L_SSD, LANES), lambda b, c: (b * nc + c, 0)),
            const((CONV_WIDTH, N_SLABS, 1, LANES)),
            const((N_SLABS, 1, LANES)),
            const((1, LANES)),
            const((1, D_INNER)),
            const((1, D_INNER)),
            const((LANES, SSM_HEADS * L_SSD)),
            const((LANES, D_INNER)),
            const((HEADS_PER_GROUP, 1, GROUP_W)),
        ],
        out_specs=pl.BlockSpec((L_SSD, D_INNER), lambda b, c: (b * nc + c, 0)),
        out_shape=jax.ShapeDtypeStruct((t, D_INNER), BF16),
        scratch_shapes=[
            pltpu.VMEM((N_SLABS, L_SSD + HIST, LANES), F32),
            pltpu.VMEM((N_SLABS, L_SSD, LANES), F32),
            pltpu.VMEM((SSM_GROUPS, D_STATE, GROUP_W), F32),
        ],
        compiler_params=pltpu.CompilerParams(
            dimension_semantics=("parallel", "arbitrary"),
            vmem_limit_bytes=VMEM_LIMIT),
        name="ssd",
    )(proj, proj, dt, conv_w, conv_b, a_row, dskip_w, g_norm, sel, sel_wide, head_mask)


def _attn_kernel(slopes_ref, q_ref, k_ref, v_ref, lq1_ref, lk1_ref, lq2_ref, lk2_ref, gsub_ref,
                 o_ref, vt_ref, ka_ref, feat_ref, bias_diag_ref, qm_ref, s_a, s_b, cm_a, cm_b, m_ref,
                 acc_ref, knorm_ref, *, lam_init):
    T = T_ATTN
    h = pl.program_id(1)
    step = pl.program_id(2)
    slope = slopes_ref[h] * LOG2E

    lane = lax.broadcasted_iota(jnp.int32, (T, HEAD_W), 1)
    data_lanes = [(lane >= ATTN_HEAD_DIM * mi) & (lane < ATTN_HEAD_DIM * (mi + 1)) for mi in range(2)]
    data_mask = [d.astype(F32).astype(BF16) for d in data_lanes]

    @pl.when(step == 0)
    def _():
        key = lax.broadcasted_iota(jnp.int32, (T, T), 0)
        qry = lax.broadcasted_iota(jnp.int32, (T, T), 1)
        rel = (key - qry).astype(F32)
        visible = (qry // CHUNK) >= (key // CHUNK)
        bias_diag_ref[...] = jnp.where(visible, -slope * jnp.abs(rel), NEG_BIG)
        pos = lax.broadcasted_iota(jnp.int32, (T, HEAD_W), 0).astype(F32)
        sp = tuple(p.astype(F32) for p in _split3(slope * pos))
        one = jnp.ones((T, HEAD_W), F32)
        k_feats = sp + (one, one, one)
        q_feats = (one, one, one) + tuple(-p for p in sp)
        for mi in range(2):
            first = ATTN_HEAD_DIM * (1 - mi)
            fk = jnp.zeros((T, HEAD_W), F32)
            fq = jnp.zeros((T, HEAD_W), F32)
            for n in range(len(k_feats)):
                fk = jnp.where(lane == first + n, k_feats[n], fk)
                fq = jnp.where(lane == first + n, q_feats[n], fq)
            feat_ref[mi] = fk.astype(BF16)
            feat_ref[2 + mi] = fq.astype(BF16)
        k_sq = [jnp.zeros((T, 1), F32), jnp.zeros((T, 1), F32)]
        for c in range(vt_ref.shape[0]):
            kc = k_ref[c * T:(c + 1) * T, :]
            kf = kc.astype(F32)
            for mi in range(2):
                ka_ref[mi, c] = kc * data_mask[mi] + feat_ref[mi]
                half = jnp.where(data_lanes[mi], kf * kf, 0.0)
                k_sq[mi] = jnp.maximum(k_sq[mi], jnp.sum(half, axis=1, keepdims=True))
        for mi in range(2):
            knorm_ref[mi] = jnp.sqrt(jnp.max(k_sq[mi]))
        ones_rows = (lax.broadcasted_iota(jnp.int32, (VT_ROWS - HEAD_W, T), 0) == 0).astype(BF16)
        for c in range(vt_ref.shape[0]):
            vt_ref[c, 0:HEAD_W, :] = v_ref[c * T:(c + 1) * T, :].astype(F32).T.astype(BF16)
            vt_ref[c, HEAD_W:VT_ROWS, :] = ones_rows

    map_rows = lax.broadcasted_iota(jnp.int32, (8, HEAD_W), 0) == lax.broadcasted_iota(
        jnp.int32, (8, HEAD_W), 1) // ATTN_HEAD_DIM
    q_sq = []
    for tile in range(2):
        q = q_ref[tile * T:(tile + 1) * T, :]
        qs = (q.astype(F32) * (ATTN_HEAD_DIM ** -0.5 * LOG2E)).astype(BF16)
        q_sq.append(_nt_dot(map_rows.astype(F32).astype(BF16), qs * qs))
        for mi in range(2):
            qm_ref[0, 2 * tile + mi] = qs * data_mask[mi]
            qm_ref[1, 2 * tile + mi] = qs * data_mask[mi] + feat_ref[2 + mi]

    m_ref[...] = jnp.full(m_ref.shape, NEG_BIG, F32)
    acc_ref[...] = jnp.zeros(acc_ref.shape, F32)

    def scores(j, tile, diagonal, s_buf, cm_buf):
        for mi in range(2):
            kj = ka_ref[mi, j]
            if diagonal:
                s = _nt_dot(kj, qm_ref[0, 2 * tile + mi]) + bias_diag_ref[...]
            else:
                s = _nt_dot(kj, qm_ref[1, 2 * tile + mi])
            s_buf[mi] = s
            cm_buf[mi] = jnp.max(s, axis=0, keepdims=True)

    def accumulate(j, tile, s_buf, cm_buf, shift):
        vtj = vt_ref[j]
        for mi in range(2):
            idx = 2 * tile + mi
            m_old = m_ref[idx]
            m_new = jnp.maximum(m_old, cm_buf[mi] + shift)
            alpha = jnp.exp2(m_old - m_new)
            p = jnp.exp2(s_buf[mi] - (m_new - shift))
            acc_ref[idx] = alpha * acc_ref[idx] + _dot(vtj, p.astype(BF16))
            m_ref[idx] = m_new

    def off_shift(j, tile):
        return slope * ((j - (2 * step + tile)) * T).astype(F32)

    d0 = 2 * step
    scores(d0, 0, True, s_a, cm_a)
    scores(d0 + 1, 1, True, s_b, cm_b)
    accumulate(d0, 0, s_a, cm_a, 0.0)

    def first_needed(tile, maxima=None):
        gap = jnp.full((1, T), NEG_BIG, F32)
        for mi in range(2):
            reach = jnp.sqrt(q_sq[tile][mi:mi + 1, :]) * (knorm_ref[mi] * NORM_SLACK)
            m_now = m_ref[2 * tile + mi] if maxima is None else maxima[mi]
            gap = jnp.maximum(gap, reach - m_now)
        x = (jnp.max(gap) + UNDERFLOW_LOG2) / slope
        dist = jnp.floor(jnp.clip(x / T, 0.0, 1e6)).astype(jnp.int32) + 2
        return (d0 + tile) - dist + 1

    def body(t, carry):
        j_b, shift_b = carry
        for j in (2 * t, 2 * t + 1):
            scores(j, 0, False, s_a, cm_a)
            accumulate(j_b, 1, s_b, cm_b, shift_b)
            scores(j, 1, False, s_b, cm_b)
            accumulate(j, 0, s_a, cm_a, off_shift(j, 0))
            j_b, shift_b = j, off_shift(j, 1)
        return j_b, shift_b

    m_tile1 = [jnp.maximum(m_ref[2 + mi], cm_b[mi]) for mi in range(2)]
    j_lo = jnp.maximum(jnp.minimum(first_needed(0), first_needed(1, m_tile1)), 0)
    j_b, shift_b = lax.fori_loop(j_lo // 2, step, body, (d0 + 1, jnp.float32(0.0)))
    scores(d0, 1, False, s_a, cm_a)
    accumulate(j_b, 1, s_b, cm_b, shift_b)
    accumulate(d0, 1, s_a, cm_a, off_shift(d0, 1))

    lam = (jnp.exp(jnp.sum(lq1_ref[...] * lk1_ref[...], axis=-1, keepdims=True))
           - jnp.exp(jnp.sum(lq2_ref[...] * lk2_ref[...], axis=-1, keepdims=True)) + lam_init)
    for tile in range(2):
        num = [acc_ref[2 * tile + mi, 0:HEAD_W, :] for mi in range(2)]
        den = [acc_ref[2 * tile + mi, HEAD_W:HEAD_W + 1, :] for mi in range(2)]
        o_t = num[0] * (1.0 / den[0]) - lam * (num[1] * (1.0 / den[1]))
        ms = jnp.mean(o_t * o_t, axis=0, keepdims=True)
        o = (o_t * lax.rsqrt(ms + EPS)).T
        o_ref[tile * T:(tile + 1) * T, :] = (o * gsub_ref[...] * (1.0 - lam_init)).astype(BF16)


def _attention(proj, slopes, lq1, lk1, lq2, lk2, g_subln, bsz, seq, lam_init):
    t = proj.shape[0]
    tq = 2 * T_ATTN
    nq = seq // tq
    vec = lambda n: pl.BlockSpec((1, n), lambda b, h, i: (0, 0))
    return pl.pallas_call(
        functools.partial(_attn_kernel, lam_init=lam_init),
        grid=(bsz, ATTN_HEADS, nq),
        in_specs=[
            pl.BlockSpec(memory_space=pltpu.SMEM),
            pl.BlockSpec((tq, HEAD_W), lambda b, h, i: (b * nq + i, COL_Q // HEAD_W + h)),
            pl.BlockSpec((seq, HEAD_W), lambda b, h, i: (b, COL_K // HEAD_W + h)),
            pl.BlockSpec((seq, HEAD_W), lambda b, h, i: (b, COL_V // HEAD_W + h)),
            vec(ATTN_HEAD_DIM), vec(ATTN_HEAD_DIM), vec(ATTN_HEAD_DIM), vec(ATTN_HEAD_DIM),
            vec(HEAD_W),
        ],
        out_specs=pl.BlockSpec((tq, HEAD_W), lambda b, h, i: (b * nq + i, h)),
        out_shape=jax.ShapeDtypeStruct((t, ATTN_HEADS * HEAD_W), BF16),
        scratch_shapes=[
            pltpu.VMEM((seq // T_ATTN, VT_ROWS, T_ATTN), BF16),
            pltpu.VMEM((2, seq // T_ATTN, T_ATTN, HEAD_W), BF16),
            pltpu.VMEM((4, T_ATTN, HEAD_W), BF16),
            pltpu.VMEM((T_ATTN, T_ATTN), F32),
            pltpu.VMEM((2, 4, T_ATTN, HEAD_W), BF16),
            pltpu.VMEM((2, T_ATTN, T_ATTN), F32),
            pltpu.VMEM((2, T_ATTN, T_ATTN), F32),
            pltpu.VMEM((2, 1, T_ATTN), F32),
            pltpu.VMEM((2, 1, T_ATTN), F32),
            pltpu.VMEM((4, 1, T_ATTN), F32),
            pltpu.VMEM((4, VT_ROWS, T_ATTN), F32),
            pltpu.SMEM((2,), F32),
        ],
        compiler_params=pltpu.CompilerParams(
            dimension_semantics=("parallel", "parallel", "arbitrary"),
            vmem_limit_bytes=VMEM_LIMIT),
        name="diff_attn",
    )(slopes, proj, proj, proj, lq1, lk1, lq2, lk2, g_subln)


def _merge_kernel(y_ref, o_ref, gs_ref, ga_ref, x_ref, bgs_ref, bga_ref, wbs_ref, wba_ref, wo_ref,
                  gn_ref, x1_ref, h2_ref):
    br_ssm = _dot(y_ref[...], wbs_ref[...])
    br_attn = _dot(o_ref[...], wba_ref[...])
    g_ssm = _sigmoid(gs_ref[...].astype(F32) + bgs_ref[...])
    g_attn = _sigmoid(ga_ref[...].astype(F32) + bga_ref[...])
    merged = (g_ssm * br_ssm + g_attn * br_attn).astype(BF16)
    x1 = x_ref[...] + _dot(merged, wo_ref[...])
    x1_ref[...] = x1
    ms = jnp.mean(x1 * x1, axis=-1, keepdims=True)
    h2_ref[...] = (x1 * lax.rsqrt(ms + EPS) * gn_ref[...]).astype(BF16)


def _merge(y_ssm, o_attn, proj, x2, bg_ssm, bg_attn, w_bs, w_ba, w_o, g_mlp):
    t = x2.shape[0]
    rows = lambda n: pl.BlockSpec((TM_MERGE, n), lambda i: (i, 0))
    const = lambda shape: pl.BlockSpec(shape, lambda i: (0, 0))
    return pl.pallas_call(
        _merge_kernel,
        grid=(t // TM_MERGE,),
        in_specs=[
            rows(D_INNER),
            rows(D_MODEL),
            pl.BlockSpec((TM_MERGE, D_MODEL), lambda i: (i, COL_GATE // D_MODEL)),
            pl.BlockSpec((TM_MERGE, D_MODEL), lambda i: (i, COL_GATE // D_MODEL + 1)),
            rows(D_MODEL),
            const((1, D_MODEL)), const((1, D_MODEL)),
            const((D_INNER, D_MODEL)), const((D_MODEL, D_MODEL)), const((D_MODEL, D_MODEL)),
            const((1, D_MODEL)),
        ],
        out_specs=[rows(D_MODEL), rows(D_MODEL)],
        out_shape=[jax.ShapeDtypeStruct((t, D_MODEL), F32), jax.ShapeDtypeStruct((t, D_MODEL), BF16)],
        compiler_params=pltpu.CompilerParams(
            dimension_semantics=("parallel",), vmem_limit_bytes=VMEM_LIMIT),
        name="merge",
    )(y_ssm, o_attn, proj, proj, x2, bg_ssm, bg_attn, w_bs, w_ba, w_o, g_mlp)


def _mlp_kernel(x1_ref, h2_ref, wu_ref, wd_ref, gf_ref, out_ref):
    h2 = h2_ref[...]
    acc = x1_ref[...]
    for c in range(D_FF // FF_CHUNK):
        u = jnp.maximum(_dot(h2, wu_ref[:, c * FF_CHUNK:(c + 1) * FF_CHUNK]), 0.0)
        acc = acc + _dot((u * u).astype(BF16), wd_ref[c * FF_CHUNK:(c + 1) * FF_CHUNK, :])
    ms = jnp.mean(acc * acc, axis=-1, keepdims=True)
    out_ref[...] = acc * lax.rsqrt(ms + EPS) * gf_ref[...]


def _mlp(x1, h2, w_up, w_down, g_final):
    t = x1.shape[0]
    rows = lambda n: pl.BlockSpec((TM_MLP, n), lambda i: (i, 0))
    const = lambda shape: pl.BlockSpec(shape, lambda i: (0, 0), pipeline_mode=pl.Buffered(1))
    return pl.pallas_call(
        _mlp_kernel,
        grid=(t // TM_MLP,),
        in_specs=[rows(D_MODEL), rows(D_MODEL), const((D_MODEL, D_FF)), const((D_FF, D_MODEL)),
                  const((1, D_MODEL))],
        out_specs=rows(D_MODEL),
        out_shape=jax.ShapeDtypeStruct((t, D_MODEL), F32),
        compiler_params=pltpu.CompilerParams(
            dimension_semantics=("parallel",), vmem_limit_bytes=VMEM_LIMIT),
        name="mlp",
    )(x1, h2, w_up, w_down, g_final)


def kernel(x, g_norm_mix, w_in, b_gate, conv_w, conv_b, dt_bias, a_log, d_skip, g_ssm_norm,
           lambda_q1, lambda_k1, lambda_q2, lambda_k2, g_subln, w_br_ssm, w_br_attn, w_out,
           g_norm_mlp, w_up, w_down, g_norm_final):
    bsz, seq, _ = x.shape
    depth = w_in.shape[0]
    x2 = x.reshape(bsz * seq, D_MODEL)

    lane_head = jnp.arange(LANES)[:, None] % SSM_HEADS
    live = jnp.arange(LANES)[:, None] < 3 * SSM_HEADS
    sel = (live & (lane_head == jnp.arange(SSM_HEADS * L_SSD)[None, :] // L_SSD)).astype(BF16)
    sel_wide = (live & (lane_head == jnp.arange(D_INNER)[None, :] // SSM_HEAD_DIM)).astype(BF16)
    head_mask = (jnp.arange(GROUP_W)[None, None, :] // SSM_HEAD_DIM
                 == jnp.arange(HEADS_PER_GROUP)[:, None, None]).astype(BF16)
    slopes = jnp.exp2(-8.0 * jnp.arange(1, ATTN_HEADS + 1, dtype=F32) / ATTN_HEADS)

    for l in range(depth):
        w = w_in[l]
        s_z, s_xbc, s_dt = D_INNER, D_INNER + CONV_DIM, D_INNER + CONV_DIM + SSM_HEADS
        w_slab = jnp.concatenate([w[:, s_z:s_xbc], 0.5 * w[:, :s_z], w[:, s_dt:]], axis=1).astype(BF16)
        rep = LANES // SSM_HEADS
        w_dt = jnp.tile(w[:, s_xbc:s_dt], (1, rep)).astype(BF16)
        dtb = jnp.tile(dt_bias[l], rep).reshape(1, LANES)
        a_row = jnp.tile(-jnp.exp(a_log[l].astype(F32)) * LOG2E, rep).reshape(1, LANES)
        dskip_w = jnp.repeat(d_skip[l], SSM_HEAD_DIM).reshape(1, D_INNER)
        lam_init = 0.8 - 0.6 * math.exp(-0.3 * l)

        proj, dt = _inproj(x2, g_norm_mix[l].reshape(1, D_MODEL), w_slab, w_dt, dtb)
        y_ssm = _ssd(proj, dt, (0.5 * conv_w[l]).reshape(CONV_WIDTH, N_SLABS, 1, LANES),
                     (0.5 * conv_b[l]).reshape(N_SLABS, 1, LANES), a_row, dskip_w,
                     g_ssm_norm[l].reshape(1, D_INNER), sel, sel_wide, head_mask, bsz, seq)
        o_attn = _attention(proj, slopes,
                            lambda_q1[l].reshape(1, -1), lambda_k1[l].reshape(1, -1),
                            lambda_q2[l].reshape(1, -1), lambda_k2[l].reshape(1, -1),
                            g_subln[l].reshape(1, HEAD_W), bsz, seq, lam_init)
        x1, h2 = _merge(y_ssm, o_attn, proj, x2,
                        b_gate[l, :D_MODEL].reshape(1, D_MODEL), b_gate[l, D_MODEL:].reshape(1, D_MODEL),
                        w_br_ssm[l].astype(BF16), w_br_attn[l].astype(BF16), w_out[l].astype(BF16),
                        g_norm_mlp[l].reshape(1, D_MODEL))
        assert l == depth - 1
        x2 = _mlp(x1, h2, w_up[l].astype(BF16), w_down[l].astype(BF16),
                  g_norm_final.reshape(1, D_MODEL))
    return x2.reshape(bsz, seq, D_MODEL)
```

```python
import functools
import math

import jax
import jax.numpy as jnp
from jax import lax
from jax.experimental import pallas as pl
from jax.experimental.pallas import tpu as pltpu

F32 = jnp.float32
BF16 = jnp.bfloat16

D_MODEL = 1024
CHUNK = 64
D_INNER = 2048
SSM_HEAD_DIM = 64
SSM_HEADS = 32
SSM_GROUPS = 8
HEADS_PER_GROUP = 4
D_STATE = 128
CONV_WIDTH = 4
CONV_DIM = 4096
GROUP_W = D_INNER // SSM_GROUPS
ATTN_HEADS = 8
ATTN_HEAD_DIM = 64
HEAD_W = 2 * ATTN_HEAD_DIM
VT_ROWS = HEAD_W + 16
D_FF = 4096
EPS = 1e-5
LOG2E = 1.4426950408889634
NEG_BIG = -1e30
UNDERFLOW_LOG2 = 152.0
NORM_SLACK = 1.02

LANES = 128
N_SLABS = CONV_DIM // LANES
HIST = 8

COL_XBC = 0
COL_Z = COL_XBC + CONV_DIM
COL_Q = COL_Z + D_INNER
COL_K = COL_Q + ATTN_HEADS * HEAD_W
COL_V = COL_K + ATTN_HEADS * HEAD_W
COL_GATE = COL_V + ATTN_HEADS * HEAD_W
N_SLAB = COL_GATE + 2 * D_MODEL

TM_PROJ = 1024
TN_PROJ = 2816
L_SSD = 128
T_ATTN = 512
TM_MERGE = 512
TM_MLP = 512
FF_CHUNK = 1024

VMEM_LIMIT = 56 * 1024 * 1024


def _nt_dot(a, b):
    return lax.dot_general(a, b, (((1,), (1,)), ((), ())), preferred_element_type=F32)


def _dot(a, b):
    return jnp.dot(a, b, preferred_element_type=F32)


def _dot_f32(a, b):
    return jnp.dot(a, b, preferred_element_type=F32, precision=lax.Precision.HIGHEST)


def _sigmoid(x):
    return 1.0 / (1.0 + jnp.exp(-x))


def _split3(x):
    p1 = x.astype(BF16)
    r1 = x - p1.astype(F32)
    p2 = r1.astype(BF16)
    p3 = (r1 - p2.astype(F32)).astype(BF16)
    return p1, p2, p3


def _inproj_kernel(x_ref, g_ref, w_ref, wdt_ref, dtb_ref, o_ref, dt_ref, h_ref):
    @pl.when(pl.program_id(1) == 0)
    def _():
        x = x_ref[...]
        ms = jnp.mean(x * x, axis=-1, keepdims=True)
        h = (x * lax.rsqrt(ms + EPS) * g_ref[...]).astype(BF16)
        h_ref[...] = h
        dt_raw = _dot(h, wdt_ref[...]) + dtb_ref[...]
        dt_ref[...] = jnp.maximum(dt_raw, 0.0) + jnp.log(1.0 + jnp.exp(-jnp.abs(dt_raw)))

    o_ref[...] = _dot(h_ref[...], w_ref[...]).astype(BF16)


def _inproj(x2, g, w_slab, w_dt, dt_bias):
    t = x2.shape[0]
    grid = (t // TM_PROJ, N_SLAB // TN_PROJ)
    return pl.pallas_call(
        _inproj_kernel,
        grid=grid,
        in_specs=[
            pl.BlockSpec((TM_PROJ, D_MODEL), lambda i, j: (i, 0)),
            pl.BlockSpec((1, D_MODEL), lambda i, j: (0, 0)),
            pl.BlockSpec((D_MODEL, TN_PROJ), lambda i, j: (0, j)),
            pl.BlockSpec((D_MODEL, LANES), lambda i, j: (0, 0)),
            pl.BlockSpec((1, LANES), lambda i, j: (0, 0)),
        ],
        out_specs=[
            pl.BlockSpec((TM_PROJ, TN_PROJ), lambda i, j: (i, j)),
            pl.BlockSpec((TM_PROJ, LANES), lambda i, j: (i, 0)),
        ],
        out_shape=[
            jax.ShapeDtypeStruct((t, N_SLAB), BF16),
            jax.ShapeDtypeStruct((t, LANES), F32),
        ],
        scratch_shapes=[pltpu.VMEM((TM_PROJ, D_MODEL), BF16)],
        compiler_params=pltpu.CompilerParams(
            dimension_semantics=("parallel", "arbitrary"),
            vmem_limit_bytes=VMEM_LIMIT),
        name="inproj",
    )(x2, g, w_slab, w_dt, dt_bias)


def _silu_of_half(hx):
    return hx + hx * jnp.tanh(hx)


def _ssd_kernel(xbc_ref, z_ref, dt_ref, cw_ref, cb_ref, a_ref, dsk_ref, gn_ref, sel_ref, selw_ref,
                hm_ref, o_ref, xpad_ref, xc_ref, state_ref):
    L = L_SSD

    @pl.when(pl.program_id(1) == 0)
    def _():
        state_ref[...] = jnp.zeros_like(state_ref)
        xpad_ref[:, 0:HIST, :] = jnp.zeros((N_SLABS, HIST, LANES), F32)

    for c in range(N_SLABS):
        xpad_ref[c, HIST:HIST + L, :] = xbc_ref[:, c * LANES:(c + 1) * LANES].astype(F32)
    for c in range(N_SLABS):
        taps = [cw_ref[k, c] for k in range(CONV_WIDTH)]
        bias = cb_ref[c]
        for s in [32 * b + j for b in range(L // 32) for j in range(4)]:
            acc = bias
            for k in range(CONV_WIDTH):
                first = HIST + s - (CONV_WIDTH - 1) + k
                acc = acc + xpad_ref[pl.ds(c, 1), pl.ds(first, 8, stride=4), :][0] * taps[k]
            xc_ref[pl.ds(c, 1), pl.ds(s, 8, stride=4), :] = _silu_of_half(acc)[None]
    xpad_ref[:, 0:HIST, :] = xpad_ref[:, L:L + HIST, :]

    dt = dt_ref[...]
    row = lax.broadcasted_iota(jnp.int32, (L, L), 0)
    col = lax.broadcasted_iota(jnp.int32, (L, L), 1)
    tri = row >= col
    acs = _dot_f32(tri.astype(F32), dt * a_ref[...])
    acs_last = acs[L - 1:L, :]
    src_t = (acs - jnp.log2(dt)).T
    to_end_t = jnp.exp2(acs_last - acs + jnp.log2(dt)).T

    a1, a2, a3 = _split3(acs)
    quarter = lax.broadcasted_iota(jnp.int32, (L, LANES), 1) // SSM_HEADS
    pieces = jnp.where(quarter == 0, a1, jnp.where(quarter == 1, a2, jnp.where(quarter == 2, a3, 0)))
    chunk_decay = jnp.exp2(_dot(pieces[L - 8:L, :], selw_ref[...]))[7:8, :]

    for g in range(SSM_GROUPS):
        ch = slice(g * GROUP_W, (g + 1) * GROUP_W)
        xg = jnp.concatenate([xc_ref[2 * g], xc_ref[2 * g + 1]], axis=1)
        bg = xc_ref[D_INNER // LANES + g]
        cg = xc_ref[D_INNER // LANES + SSM_GROUPS + g]
        cb = _nt_dot(cg.astype(BF16), bg.astype(BF16))
        gw = HEADS_PER_GROUP * L
        acs_bc = _dot(pieces, sel_ref[:, g * gw:(g + 1) * gw])
        bg_t = bg.T
        st = state_ref[g]
        xb = xg.astype(BF16)
        lhs, rhs, bts = [], [], []
        for r in range(HEADS_PER_GROUP):
            hd = g * HEADS_PER_GROUP + r
            bc = acs_bc[:, r * L:(r + 1) * L]
            w = cb * jnp.where(tri, jnp.exp2(bc - src_t[hd:hd + 1, :]), 0.0)
            lhs.append(w.astype(BF16))
            rhs.append(xb * hm_ref[r])
            bts.append((bg_t * to_end_t[hd:hd + 1, :]).astype(BF16))
        x_heads = jnp.concatenate(rhs, axis=0)
        since_start = jnp.exp2(_dot(pieces, selw_ref[:, ch]))
        y = (xg * dsk_ref[:, ch] + _dot(jnp.concatenate(lhs, axis=1), x_heads)
             + _dot(cg.astype(BF16), st.astype(BF16)) * since_start)
        state_ref[g] = st * chunk_decay[:, ch] + _dot(jnp.concatenate(bts, axis=1), x_heads)

        y = y * _silu_of_half(z_ref[:, ch].astype(F32))
        ms = jnp.mean(y * y, axis=-1, keepdims=True)
        o_ref[:, ch] = (y * lax.rsqrt(ms + EPS) * gn_ref[:, ch]).astype(BF16)


def _ssd(proj, dt, conv_w, conv_b, a_row, dskip_w, g_norm, sel, sel_wide, head_mask, bsz, seq):
    t = proj.shape[0]
    nc = seq // L_SSD
    const = lambda shape: pl.BlockSpec(shape, lambda b, c: (0,) * len(shape))
    return pl.pallas_call(
        _ssd_kernel,
        grid=(bsz, nc),
        in_specs=[
            pl.BlockSpec((L_SSD, CONV_DIM), lambda b, c: (b * nc + c, COL_XBC // CONV_DIM)),
            pl.BlockSpec((L_SSD, D_INNER), lambda b, c: (b * nc + c, COL_Z // D_INNER)),
            pl.BlockSpec((L_SSD, LANES), lambda b, c: (b * nc + c, 0)),
            const((CONV_WIDTH, N_SLABS, 1, LANES)),
            const((N_SLABS, 1, LANES)),
            const((1, LANES)),
            const((1, D_INNER)),
            const((1, D_INNER)),
            const((LANES, SSM_HEADS * L_SSD)),
            const((LANES, D_INNER)),
            const((HEADS_PER_GROUP, 1, GROUP_W)),
        ],
        out_specs=pl.BlockSpec((L_SSD, D_INNER), lambda b, c: (b * nc + c, 0)),
        out_shape=jax.ShapeDtypeStruct((t, D_INNER), BF16),
        scratch_shapes=[
            pltpu.VMEM((N_SLABS, L_SSD + HIST, LANES), F32),
            pltpu.VMEM((N_SLABS, L_SSD, LANES), F32),
            pltpu.VMEM((SSM_GROUPS, D_STATE, GROUP_W), F32),
        ],
        compiler_params=pltpu.CompilerParams(
            dimension_semantics=("parallel", "arbitrary"),
            vmem_limit_bytes=VMEM_LIMIT),
        name="ssd",
    )(proj, proj, dt, conv_w, conv_b, a_row, dskip_w, g_norm, sel, sel_wide, head_mask)


def _attn_kernel(slopes_ref, q_ref, k_ref, v_ref, lq1_ref, lk1_ref, lq2_ref, lk2_ref, gsub_ref,
                 o_ref, vt_ref, ka_ref, feat_ref, bias_diag_ref, qm_ref, s_a, s_b, cm_a, cm_b, m_ref,
                 acc_ref, knorm_ref, *, lam_init):
    T = T_ATTN
    h = pl.program_id(1)
    step = pl.program_id(2)
    slope = slopes_ref[h] * LOG2E

    lane = lax.broadcasted_iota(jnp.int32, (T, HEAD_W), 1)
    data_lanes = [(lane >= ATTN_HEAD_DIM * mi) & (lane < ATTN_HEAD_DIM * (mi + 1)) for mi in range(2)]
    data_mask = [d.astype(F32).astype(BF16) for d in data_lanes]

    @pl.when(step == 0)
    def _():
        key = lax.broadcasted_iota(jnp.int32, (T, T), 0)
        qry = lax.broadcasted_iota(jnp.int32, (T, T), 1)
        rel = (key - qry).astype(F32)
        visible = (qry // CHUNK) >= (key // CHUNK)
        bias_diag_ref[...] = jnp.where(visible, -slope * jnp.abs(rel), NEG_BIG)
        pos = lax.broadcasted_iota(jnp.int32, (T, HEAD_W), 0).astype(F32)
        sp = tuple(p.astype(F32) for p in _split3(slope * pos))
        one = jnp.ones((T, HEAD_W), F32)
        k_feats = sp + (one, one, one)
        q_feats = (one, one, one) + tuple(-p for p in sp)
        for mi in range(2):
            first = ATTN_HEAD_DIM * (1 - mi)
            fk = jnp.zeros((T, HEAD_W), F32)
            fq = jnp.zeros((T, HEAD_W), F32)
            for n in range(len(k_feats)):
                fk = jnp.where(lane == first + n, k_feats[n], fk)
                fq = jnp.where(lane == first + n, q_feats[n], fq)
            feat_ref[mi] = fk.astype(BF16)
            feat_ref[2 + mi] = fq.astype(BF16)
        k_sq = [jnp.zeros((T, 1), F32), jnp.zeros((T, 1), F32)]
        for c in range(vt_ref.shape[0]):
            kc = k_ref[c * T:(c + 1) * T, :]
            kf = kc.astype(F32)
            for mi in range(2):
                ka_ref[mi, c] = kc * data_mask[mi] + feat_ref[mi]
                half = jnp.where(data_lanes[mi], kf * kf, 0.0)
                k_sq[mi] = jnp.maximum(k_sq[mi], jnp.sum(half, axis=1, keepdims=True))
        for mi in range(2):
            knorm_ref[mi] = jnp.sqrt(jnp.max(k_sq[mi]))
        ones_rows = (lax.broadcasted_iota(jnp.int32, (VT_ROWS - HEAD_W, T), 0) == 0).astype(BF16)
        for c in range(vt_ref.shape[0]):
            vt_ref[c, 0:HEAD_W, :] = v_ref[c * T:(c + 1) * T, :].astype(F32).T.astype(BF16)
            vt_ref[c, HEAD_W:VT_ROWS, :] = ones_rows

    map_rows = lax.broadcasted_iota(jnp.int32, (8, HEAD_W), 0) == lax.broadcasted_iota(
        jnp.int32, (8, HEAD_W), 1) // ATTN_HEAD_DIM
    q_sq = []
    for tile in range(2):
        q = q_ref[tile * T:(tile + 1) * T, :]
        qs = (q.astype(F32) * (ATTN_HEAD_DIM ** -0.5 * LOG2E)).astype(BF16)
        q_sq.append(_nt_dot(map_rows.astype(F32).astype(BF16), qs * qs))
        for mi in range(2):
            qm_ref[0, 2 * tile + mi] = qs * data_mask[mi]
            qm_ref[1, 2 * tile + mi] = qs * data_mask[mi] + feat_ref[2 + mi]

    m_ref[...] = jnp.full(m_ref.shape, NEG_BIG, F32)
    acc_ref[...] = jnp.zeros(acc_ref.shape, F32)

    def scores(j, tile, diagonal, s_buf, cm_buf):
        for mi in range(2):
            kj = ka_ref[mi, j]
            if diagonal:
                s = _nt_dot(kj, qm_ref[0, 2 * tile + mi]) + bias_diag_ref[...]
            else:
                s = _nt_dot(kj, qm_ref[1, 2 * tile + mi])
            s_buf[mi] = s
            cm_buf[mi] = jnp.max(s, axis=0, keepdims=True)

    def accumulate(j, tile, s_buf, cm_buf, shift):
        vtj = vt_ref[j]
        for mi in range(2):
            idx = 2 * tile + mi
            m_old = m_ref[idx]
            m_new = jnp.maximum(m_old, cm_buf[mi] + shift)
            alpha = jnp.exp2(m_old - m_new)
            p = jnp.exp2(s_buf[mi] - (m_new - shift))
            acc_ref[idx] = alpha * acc_ref[idx] + _dot(vtj, p.astype(BF16))
            m_ref[idx] = m_new

    def off_shift(j, tile):
        return slope * ((j - (2 * step + tile)) * T).astype(F32)

    def reach_blocks(tile, diag_max):
        gap = jnp.full((1, T), NEG_BIG, F32)
        for mi in range(2):
            reach = jnp.sqrt(q_sq[tile][mi:mi + 1, :]) * (knorm_ref[mi] * NORM_SLACK)
            gap = jnp.maximum(gap, reach - diag_max[mi])
        x = (jnp.max(gap) + UNDERFLOW_LOG2) / slope
        return jnp.floor(jnp.clip(x / T, 0.0, 1e6)).astype(jnp.int32) + 1

    d0 = 2 * step
    scores(d0, 0, True, s_a, cm_a)
    scores(d0 + 1, 1, True, s_b, cm_b)
    reach1 = reach_blocks(1, cm_b)
    needed = jnp.minimum(jnp.maximum(reach_blocks(0, cm_a), reach1), d0)
    accumulate(d0, 0, s_a, cm_a, 0.0)

    def body(t, carry):
        j_b, shift_b = carry
        for back in (2 * t + 1, 2 * t + 2):
            j0, j1 = d0 - back, d0 + 1 - back
            scores(j0, 0, False, s_a, cm_a)
            accumulate(j_b, 1, s_b, cm_b, shift_b)
            scores(j1, 1, False, s_b, cm_b)
            accumulate(j0, 0, s_a, cm_a, off_shift(j0, 0))
            j_b, shift_b = j1, off_shift(j1, 1)
        return j_b, shift_b

    j_b, shift_b = lax.fori_loop(0, (needed + 1) // 2, body, (d0 + 1, jnp.float32(0.0)))

    @pl.when(reach1 > d0)
    def _():
        scores(0, 1, False, s_a, cm_a)
        accumulate(j_b, 1, s_b, cm_b, shift_b)
        accumulate(0, 1, s_a, cm_a, off_shift(0, 1))

    @pl.when(reach1 <= d0)
    def _():
        accumulate(j_b, 1, s_b, cm_b, shift_b)

    lam = (jnp.exp(jnp.sum(lq1_ref[...] * lk1_ref[...], axis=-1, keepdims=True))
           - jnp.exp(jnp.sum(lq2_ref[...] * lk2_ref[...], axis=-1, keepdims=True)) + lam_init)
    for tile in range(2):
        num = [acc_ref[2 * tile + mi, 0:HEAD_W, :] for mi in range(2)]
        den = [acc_ref[2 * tile + mi, HEAD_W:HEAD_W + 1, :] for mi in range(2)]
        o_t = num[0] * (1.0 / den[0]) - lam * (num[1] * (1.0 / den[1]))
        ms = jnp.mean(o_t * o_t, axis=0, keepdims=True)
        o = (o_t * lax.rsqrt(ms + EPS)).T
        o_ref[tile * T:(tile + 1) * T, :] = (o * gsub_ref[...] * (1.0 - lam_init)).astype(BF16)


def _attention(proj, slopes, lq1, lk1, lq2, lk2, g_subln, bsz, seq, lam_init):
    t = proj.shape[0]
    tq = 2 * T_ATTN
    nq = seq // tq
    vec = lambda n: pl.BlockSpec((1, n), lambda b, h, i: (0, 0))
    return pl.pallas_call(
        functools.partial(_attn_kernel, lam_init=lam_init),
        grid=(bsz, ATTN_HEADS, nq),
        in_specs=[
            pl.BlockSpec(memory_space=pltpu.SMEM),
            pl.BlockSpec((tq, HEAD_W), lambda b, h, i: (b * nq + i, COL_Q // HEAD_W + h)),
            pl.BlockSpec((seq, HEAD_W), lambda b, h, i: (b, COL_K // HEAD_W + h)),
            pl.BlockSpec((seq, HEAD_W), lambda b, h, i: (b, COL_V // HEAD_W + h)),
            vec(ATTN_HEAD_DIM), vec(ATTN_HEAD_DIM), vec(ATTN_HEAD_DIM), vec(ATTN_HEAD_DIM),
            vec(HEAD_W),
        ],
        out_specs=pl.BlockSpec((tq, HEAD_W), lambda b, h, i: (b * nq + i, h)),
        out_shape=jax.ShapeDtypeStruct((t, ATTN_HEADS * HEAD_W), BF16),
        scratch_shapes=[
            pltpu.VMEM((seq // T_ATTN, VT_ROWS, T_ATTN), BF16),
            pltpu.VMEM((2, seq // T_ATTN, T_ATTN, HEAD_W), BF16),
            pltpu.VMEM((4, T_ATTN, HEAD_W), BF16),
            pltpu.VMEM((T_ATTN, T_ATTN), F32),
            pltpu.VMEM((2, 4, T_ATTN, HEAD_W), BF16),
            pltpu.VMEM((2, T_ATTN, T_ATTN), F32),
            pltpu.VMEM((2, T_ATTN, T_ATTN), F32),
            pltpu.VMEM((2, 1, T_ATTN), F32),
            pltpu.VMEM((2, 1, T_ATTN), F32),
            pltpu.VMEM((4, 1, T_ATTN), F32),
            pltpu.VMEM((4, VT_ROWS, T_ATTN), F32),
            pltpu.SMEM((2,), F32),
        ],
        compiler_params=pltpu.CompilerParams(
            dimension_semantics=("parallel", "parallel", "arbitrary"),
            vmem_limit_bytes=VMEM_LIMIT),
        name="diff_attn",
    )(slopes, proj, proj, proj, lq1, lk1, lq2, lk2, g_subln)


def _merge_kernel(y_ref, o_ref, gs_ref, ga_ref, x_ref, bgs_ref, bga_ref, wbs_ref, wba_ref, wo_ref,
                  gn_ref, x1_ref, h2_ref):
    br_ssm = _dot(y_ref[...], wbs_ref[...])
    br_attn = _dot(o_ref[...], wba_ref[...])
    g_ssm = _sigmoid(gs_ref[...].astype(F32) + bgs_ref[...])
    g_attn = _sigmoid(ga_ref[...].astype(F32) + bga_ref[...])
    merged = (g_ssm * br_ssm + g_attn * br_attn).astype(BF16)
    x1 = x_ref[...] + _dot(merged, wo_ref[...])
    x1_ref[...] = x1
    ms = jnp.mean(x1 * x1, axis=-1, keepdims=True)
    h2_ref[...] = (x1 * lax.rsqrt(ms + EPS) * gn_ref[...]).astype(BF16)


def _merge(y_ssm, o_attn, proj, x2, bg_ssm, bg_attn, w_bs, w_ba, w_o, g_mlp):
    t = x2.shape[0]
    rows = lambda n: pl.BlockSpec((TM_MERGE, n), lambda i: (i, 0))
    const = lambda shape: pl.BlockSpec(shape, lambda i: (0, 0))
    return pl.pallas_call(
        _merge_kernel,
        grid=(t // TM_MERGE,),
        in_specs=[
            rows(D_INNER),
            rows(D_MODEL),
            pl.BlockSpec((TM_MERGE, D_MODEL), lambda i: (i, COL_GATE // D_MODEL)),
            pl.BlockSpec((TM_MERGE, D_MODEL), lambda i: (i, COL_GATE // D_MODEL + 1)),
            rows(D_MODEL),
            const((1, D_MODEL)), const((1, D_MODEL)),
            const((D_INNER, D_MODEL)), const((D_MODEL, D_MODEL)), const((D_MODEL, D_MODEL)),
            const((1, D_MODEL)),
        ],
        out_specs=[rows(D_MODEL), rows(D_MODEL)],
        out_shape=[jax.ShapeDtypeStruct((t, D_MODEL), F32), jax.ShapeDtypeStruct((t, D_MODEL), BF16)],
        compiler_params=pltpu.CompilerParams(
            dimension_semantics=("parallel",), vmem_limit_bytes=VMEM_LIMIT),
        name="merge",
    )(y_ssm, o_attn, proj, proj, x2, bg_ssm, bg_attn, w_bs, w_ba, w_o, g_mlp)


def _mlp_kernel(x1_ref, h2_ref, wu_ref, wd_ref, gf_ref, out_ref):
    h2 = h2_ref[...]
    acc = x1_ref[...]
    for c in range(D_FF // FF_CHUNK):
        u = jnp.maximum(_dot(h2, wu_ref[:, c * FF_CHUNK:(c + 1) * FF_CHUNK]), 0.0)
        acc = acc + _dot((u * u).astype(BF16), wd_ref[c * FF_CHUNK:(c + 1) * FF_CHUNK, :])
    ms = jnp.mean(acc * acc, axis=-1, keepdims=True)
    out_ref[...] = acc * lax.rsqrt(ms + EPS) * gf_ref[...]


def _mlp(x1, h2, w_up, w_down, g_final):
    t = x1.shape[0]
    rows = lambda n: pl.BlockSpec((TM_MLP, n), lambda i: (i, 0))
    const = lambda shape: pl.BlockSpec(shape, lambda i: (0, 0), pipeline_mode=pl.Buffered(1))
    return pl.pallas_call(
        _mlp_kernel,
        grid=(t // TM_MLP,),
        in_specs=[rows(D_MODEL), rows(D_MODEL), const((D_MODEL, D_FF)), const((D_FF, D_MODEL)),
                  const((1, D_MODEL))],
        out_specs=rows(D_MODEL),
        out_shape=jax.ShapeDtypeStruct((t, D_MODEL), F32),
        compiler_params=pltpu.CompilerParams(
            dimension_semantics=("parallel",), vmem_limit_bytes=VMEM_LIMIT),
        name="mlp",
    )(x1, h2, w_up, w_down, g_final)


def kernel(x, g_norm_mix, w_in, b_gate, conv_w, conv_b, dt_bias, a_log, d_skip, g_ssm_norm,
           lambda_q1, lambda_k1, lambda_q2, lambda_k2, g_subln, w_br_ssm, w_br_attn, w_out,
           g_norm_mlp, w_up, w_down, g_norm_final):
    bsz, seq, _ = x.shape
    depth = w_in.shape[0]
    x2 = x.reshape(bsz * seq, D_MODEL)

    lane_head = jnp.arange(LANES)[:, None] % SSM_HEADS
    live = jnp.arange(LANES)[:, None] < 3 * SSM_HEADS
    sel = (live & (lane_head == jnp.arange(SSM_HEADS * L_SSD)[None, :] // L_SSD)).astype(BF16)
    sel_wide = (live & (lane_head == jnp.arange(D_INNER)[None, :] // SSM_HEAD_DIM)).astype(BF16)
    head_mask = (jnp.arange(GROUP_W)[None, None, :] // SSM_HEAD_DIM
                 == jnp.arange(HEADS_PER_GROUP)[:, None, None]).astype(BF16)
    slopes = jnp.exp2(-8.0 * jnp.arange(1, ATTN_HEADS + 1, dtype=F32) / ATTN_HEADS)

    for l in range(depth):
        w = w_in[l]
        s_z, s_xbc, s_dt = D_INNER, D_INNER + CONV_DIM, D_INNER + CONV_DIM + SSM_HEADS
        w_slab = jnp.concatenate([w[:, s_z:s_xbc], 0.5 * w[:, :s_z], w[:, s_dt:]], axis=1).astype(BF16)
        rep = LANES // SSM_HEADS
        w_dt = jnp.tile(w[:, s_xbc:s_dt], (1, rep)).astype(BF16)
        dtb = jnp.tile(dt_bias[l], rep).reshape(1, LANES)
        a_row = jnp.tile(-jnp.exp(a_log[l].astype(F32)) * LOG2E, rep).reshape(1, LANES)
        dskip_w = jnp.repeat(d_skip[l], SSM_HEAD_DIM).reshape(1, D_INNER)
        lam_init = 0.8 - 0.6 * math.exp(-0.3 * l)

        proj, dt = _inproj(x2, g_norm_mix[l].reshape(1, D_MODEL), w_slab, w_dt, dtb)
        y_ssm = _ssd(proj, dt, (0.5 * conv_w[l]).reshape(CONV_WIDTH, N_SLABS, 1, LANES),
                     (0.5 * conv_b[l]).reshape(N_SLABS, 1, LANES), a_row, dskip_w,
                     g_ssm_norm[l].reshape(1, D_INNER), sel, sel_wide, head_mask, bsz, seq)
        o_attn = _attention(proj, slopes,
                            lambda_q1[l].reshape(1, -1), lambda_k1[l].reshape(1, -1),
                            lambda_q2[l].reshape(1, -1), lambda_k2[l].reshape(1, -1),
                            g_subln[l].reshape(1, HEAD_W), bsz, seq, lam_init)
        x1, h2 = _merge(y_ssm, o_attn, proj, x2,
                        b_gate[l, :D_MODEL].reshape(1, D_MODEL), b_gate[l, D_MODEL:].reshape(1, D_MODEL),
                        w_br_ssm[l].astype(BF16), w_br_attn[l].astype(BF16), w_out[l].astype(BF16),
                        g_norm_mlp[l].reshape(1, D_MODEL))
        assert l == depth - 1
        x2 = _mlp(x1, h2, w_up[l].astype(BF16), w_down[l].astype(BF16),
                  g_norm_final.reshape(1, D_MODEL))
    return x2.reshape(bsz, seq, D_MODEL)
```

```python
import functools
import math

import jax
import jax.numpy as jnp
from jax import lax
from jax.experimental import pallas as pl
from jax.experimental.pallas import tpu as pltpu

F32 = jnp.float32
BF16 = jnp.bfloat16

D_MODEL = 1024
CHUNK = 64
D_INNER = 2048
SSM_HEAD_DIM = 64
SSM_HEADS = 32
SSM_GROUPS = 8
HEADS_PER_GROUP = 4
D_STATE = 128
CONV_WIDTH = 4
CONV_DIM = 4096
GROUP_W = D_INNER // SSM_GROUPS
ATTN_HEADS = 8
ATTN_HEAD_DIM = 64
HEAD_W = 2 * ATTN_HEAD_DIM
VT_ROWS = HEAD_W + 16
D_FF = 4096
EPS = 1e-5
LOG2E = 1.4426950408889634
NEG_BIG = -1e30
UNDERFLOW_LOG2 = 152.0
NORM_SLACK = 1.02

LANES = 128
N_SLABS = CONV_DIM // LANES
HIST = 8

COL_XBC = 0
COL_Z = COL_XBC + CONV_DIM
COL_Q = COL_Z + D_INNER
COL_K = COL_Q + ATTN_HEADS * HEAD_W
COL_V = COL_K + ATTN_HEADS * HEAD_W
COL_GATE = COL_V + ATTN_HEADS * HEAD_W
N_SLAB = COL_GATE + 2 * D_MODEL

TM_PROJ = 1024
TN_PROJ = 2816
L_SSD = 128
T_ATTN = 512
TM_MERGE = 512
TM_MLP = 512
FF_CHUNK = 1024

VMEM_LIMIT = 56 * 1024 * 1024


def _nt_dot(a, b):
    return lax.dot_general(a, b, (((1,), (1,)), ((), ())), preferred_element_type=F32)


def _dot(a, b):
    return jnp.dot(a, b, preferred_element_type=F32)


def _dot_f32(a, b):
    return jnp.dot(a, b, preferred_element_type=F32, precision=lax.Precision.HIGHEST)


def _sigmoid(x):
    return 1.0 / (1.0 + jnp.exp(-x))


def _split3(x):
    p1 = x.astype(BF16)
    r1 = x - p1.astype(F32)
    p2 = r1.astype(BF16)
    p3 = (r1 - p2.astype(F32)).astype(BF16)
    return p1, p2, p3


def _inproj_kernel(x_ref, g_ref, w_ref, wdt_ref, dtb_ref, o_ref, dt_ref, h_ref):
    @pl.when(pl.program_id(1) == 0)
    def _():
        x = x_ref[...]
        ms = jnp.mean(x * x, axis=-1, keepdims=True)
        h = (x * lax.rsqrt(ms + EPS) * g_ref[...]).astype(BF16)
        h_ref[...] = h
        dt_raw = _dot(h, wdt_ref[...]) + dtb_ref[...]
        dt_ref[...] = jnp.maximum(dt_raw, 0.0) + jnp.log(1.0 + jnp.exp(-jnp.abs(dt_raw)))

    o_ref[...] = _dot(h_ref[...], w_ref[...]).astype(BF16)


def _inproj(x2, g, w_slab, w_dt, dt_bias):
    t = x2.shape[0]
    grid = (t // TM_PROJ, N_SLAB // TN_PROJ)
    return pl.pallas_call(
        _inproj_kernel,
        grid=grid,
        in_specs=[
            pl.BlockSpec((TM_PROJ, D_MODEL), lambda i, j: (i, 0)),
            pl.BlockSpec((1, D_MODEL), lambda i, j: (0, 0)),
            pl.BlockSpec((D_MODEL, TN_PROJ), lambda i, j: (0, j)),
            pl.BlockSpec((D_MODEL, LANES), lambda i, j: (0, 0)),
            pl.BlockSpec((1, LANES), lambda i, j: (0, 0)),
        ],
        out_specs=[
            pl.BlockSpec((TM_PROJ, TN_PROJ), lambda i, j: (i, j)),
            pl.BlockSpec((TM_PROJ, LANES), lambda i, j: (i, 0)),
        ],
        out_shape=[
            jax.ShapeDtypeStruct((t, N_SLAB), BF16),
            jax.ShapeDtypeStruct((t, LANES), F32),
        ],
        scratch_shapes=[pltpu.VMEM((TM_PROJ, D_MODEL), BF16)],
        compiler_params=pltpu.CompilerParams(
            dimension_semantics=("parallel", "arbitrary"),
            vmem_limit_bytes=VMEM_LIMIT),
        name="inproj",
    )(x2, g, w_slab, w_dt, dt_bias)


def _silu_of_half(hx):
    return hx + hx * jnp.tanh(hx)


def _ssd_kernel(xbc_ref, z_ref, dt_ref, cw_ref, cb_ref, a_ref, dsk_ref, gn_ref, sel_ref, selw_ref,
                hm_ref, o_ref, xpad_ref, xc_ref, state_ref):
    L = L_SSD

    @pl.when(pl.program_id(1) == 0)
    def _():
        state_ref[...] = jnp.zeros_like(state_ref)
        xpad_ref[:, 0:HIST, :] = jnp.zeros((N_SLABS, HIST, LANES), F32)

    for c in range(N_SLABS):
        xpad_ref[c, HIST:HIST + L, :] = xbc_ref[:, c * LANES:(c + 1) * LANES].astype(F32)
    for c in range(N_SLABS):
        taps = [cw_ref[k, c] for k in range(CONV_WIDTH)]
        bias = cb_ref[c]
        for s in [32 * b + j for b in range(L // 32) for j in range(4)]:
            acc = bias
            for k in range(CONV_WIDTH):
                first = HIST + s - (CONV_WIDTH - 1) + k
                acc = acc + xpad_ref[pl.ds(c, 1), pl.ds(first, 8, stride=4), :][0] * taps[k]
            xc_ref[pl.ds(c, 1), pl.ds(s, 8, stride=4), :] = _silu_of_half(acc)[None]
    xpad_ref[:, 0:HIST, :] = xpad_ref[:, L:L + HIST, :]

    dt = dt_ref[...]
    row = lax.broadcasted_iota(jnp.int32, (L, L), 0)
    col = lax.broadcasted_iota(jnp.int32, (L, L), 1)
    tri = row >= col
    acs = _dot_f32(tri.astype(F32), dt * a_ref[...])
    acs_last = acs[L - 1:L, :]
    src_t = (acs - jnp.log2(dt)).T
    to_end_t = jnp.exp2(acs_last - acs + jnp.log2(dt)).T

    a1, a2, a3 = _split3(acs)
    quarter = lax.broadcasted_iota(jnp.int32, (L, LANES), 1) // SSM_HEADS
    pieces = jnp.where(quarter == 0, a1, jnp.where(quarter == 1, a2, jnp.where(quarter == 2, a3, 0)))
    chunk_decay = jnp.exp2(_dot(pieces[L - 8:L, :], selw_ref[...]))[7:8, :]

    for g in range(SSM_GROUPS):
        ch = slice(g * GROUP_W, (g + 1) * GROUP_W)
        xg = jnp.concatenate([xc_ref[2 * g], xc_ref[2 * g + 1]], axis=1)
        bg = xc_ref[D_INNER // LANES + g]
        cg = xc_ref[D_INNER // LANES + SSM_GROUPS + g]
        cb = _nt_dot(cg.astype(BF16), bg.astype(BF16))
        gw = HEADS_PER_GROUP * L
        acs_bc = _dot(pieces, sel_ref[:, g * gw:(g + 1) * gw])
        bg_t = bg.T
        st = state_ref[g]
        xb = xg.astype(BF16)
        lhs, rhs, bts = [], [], []
        for r in range(HEADS_PER_GROUP):
            hd = g * HEADS_PER_GROUP + r
            bc = acs_bc[:, r * L:(r + 1) * L]
            w = cb * jnp.where(tri, jnp.exp2(bc - src_t[hd:hd + 1, :]), 0.0)
            lhs.append(w.astype(BF16))
            rhs.append(xb * hm_ref[r])
            bts.append((bg_t * to_end_t[hd:hd + 1, :]).astype(BF16))
        x_heads = jnp.concatenate(rhs, axis=0)
        since_start = jnp.exp2(_dot(pieces, selw_ref[:, ch]))
        y = (xg * dsk_ref[:, ch] + _dot(jnp.concatenate(lhs, axis=1), x_heads)
             + _dot(cg.astype(BF16), st.astype(BF16)) * since_start)
        state_ref[g] = st * chunk_decay[:, ch] + _dot(jnp.concatenate(bts, axis=1), x_heads)

        y = y * _silu_of_half(z_ref[:, ch].astype(F32))
        ms = jnp.mean(y * y, axis=-1, keepdims=True)
        o_ref[:, ch] = (y * lax.rsqrt(ms + EPS) * gn_ref[:, ch]).astype(BF16)


def _ssd(proj, dt, conv_w, conv_b, a_row, dskip_w, g_norm, sel, sel_wide, head_mask, bsz, seq):
    t = proj.shape[0]
    nc = seq // L_SSD
    const = lambda shape: pl.BlockSpec(shape, lambda b, c: (0,) * len(shape))
    return pl.pallas_call(
        _ssd_kernel,
        grid=(bsz, nc),
        in_specs=[
            pl.BlockSpec((L_SSD, CONV_DIM), lambda b, c: (b * nc + c, COL_XBC // CONV_DIM)),
            pl.BlockSpec((L_SSD, D_INNER), lambda b, c: (b * nc + c, COL_Z // D_INNER)),
            pl.BlockSpec((L_SSD, LANES), lambda b, c: (b * nc + c, 0)),
            const((CONV_WIDTH, N_SLABS, 1, LANES)),
            const((N_SLABS, 1, LANES)),
            const((1, LANES)),
            const((1, D_INNER)),
            const((1, D_INNER)),
            const((LANES, SSM_HEADS * L_SSD)),
            const((LANES, D_INNER)),
            const((HEADS_PER_GROUP, 1, GROUP_W)),
        ],
        out_specs=pl.BlockSpec((L_SSD, D_INNER), lambda b, c: (b * nc + c, 0)),
        out_shape=jax.ShapeDtypeStruct((t, D_INNER), BF16),
        scratch_shapes=[
            pltpu.VMEM((N_SLABS, L_SSD + HIST, LANES), F32),
            pltpu.VMEM((N_SLABS, L_SSD, LANES), F32),
            pltpu.VMEM((SSM_GROUPS, D_STATE, GROUP_W), F32),
        ],
        compiler_params=pltpu.CompilerParams(
            dimension_semantics=("parallel", "arbitrary"),
            vmem_limit_bytes=VMEM_LIMIT),
        name="ssd",
    )(proj, proj, dt, conv_w, conv_b, a_row, dskip_w, g_norm, sel, sel_wide, head_mask)


def _attn_kernel(slopes_ref, q_ref, k_ref, v_ref, lq1_ref, lk1_ref, lq2_ref, lk2_ref, gsub_ref,
                 o_ref, vt_ref, ka_ref, feat_ref, bias_diag_ref, qm_ref, s_a, s_b, cm_a, cm_b, m_ref,
                 acc_ref, knorm_ref, *, lam_init):
    T = T_ATTN
    h = pl.program_id(1)
    step = pl.program_id(2)
    slope = slopes_ref[h] * LOG2E

    lane = lax.broadcasted_iota(jnp.int32, (T, HEAD_W), 1)
    data_lanes = [(lane >= ATTN_HEAD_DIM * mi) & (lane < ATTN_HEAD_DIM * (mi + 1)) for mi in range(2)]
    data_mask = [d.astype(F32).astype(BF16) for d in data_lanes]

    @pl.when(step == 0)
    def _():
        key = lax.broadcasted_iota(jnp.int32, (T, T), 0)
        qry = lax.broadcasted_iota(jnp.int32, (T, T), 1)
        rel = (key - qry).astype(F32)
        visible = (qry // CHUNK) >= (key // CHUNK)
        bias_diag_ref[...] = jnp.where(visible, -slope * jnp.abs(rel), NEG_BIG)
        pos = lax.broadcasted_iota(jnp.int32, (T, HEAD_W), 0).astype(F32)
        sp = tuple(p.astype(F32) for p in _split3(slope * pos))
        one = jnp.ones((T, HEAD_W), F32)
        k_feats = sp + (one, one, one)
        q_feats = (one, one, one) + tuple(-p for p in sp)
        for mi in range(2):
            first = ATTN_HEAD_DIM * (1 - mi)
            fk = jnp.zeros((T, HEAD_W), F32)
            fq = jnp.zeros((T, HEAD_W), F32)
            for n in range(len(k_feats)):
                fk = jnp.where(lane == first + n, k_feats[n], fk)
                fq = jnp.where(lane == first + n, q_feats[n], fq)
            feat_ref[mi] = fk.astype(BF16)
            feat_ref[2 + mi] = fq.astype(BF16)
        k_sq = [jnp.zeros((T, 1), F32), jnp.zeros((T, 1), F32)]
        for c in range(vt_ref.shape[0]):
            kc = k_ref[c * T:(c + 1) * T, :]
            kf = kc.astype(F32)
            for mi in range(2):
                ka_ref[mi, c] = kc * data_mask[mi] + feat_ref[mi]
                half = jnp.where(data_lanes[mi], kf * kf, 0.0)
                k_sq[mi] = jnp.maximum(k_sq[mi], jnp.sum(half, axis=1, keepdims=True))
        for mi in range(2):
            knorm_ref[mi] = jnp.sqrt(jnp.max(k_sq[mi]))
        ones_rows = (lax.broadcasted_iota(jnp.int32, (VT_ROWS - HEAD_W, T), 0) == 0).astype(BF16)
        for c in range(vt_ref.shape[0]):
            vt_ref[c, 0:HEAD_W, :] = v_ref[c * T:(c + 1) * T, :].astype(F32).T.astype(BF16)
            vt_ref[c, HEAD_W:VT_ROWS, :] = ones_rows

    map_rows = lax.broadcasted_iota(jnp.int32, (8, HEAD_W), 0) == lax.broadcasted_iota(
        jnp.int32, (8, HEAD_W), 1) // ATTN_HEAD_DIM
    q_sq = []
    for tile in range(2):
        q = q_ref[tile * T:(tile + 1) * T, :]
        qs = (q.astype(F32) * (ATTN_HEAD_DIM ** -0.5 * LOG2E)).astype(BF16)
        q_sq.append(_nt_dot(map_rows.astype(F32).astype(BF16), qs * qs))
        for mi in range(2):
            qm_ref[0, 2 * tile + mi] = qs * data_mask[mi]
            qm_ref[1, 2 * tile + mi] = qs * data_mask[mi] + feat_ref[2 + mi]

    m_ref[...] = jnp.full(m_ref.shape, NEG_BIG, F32)
    acc_ref[...] = jnp.zeros(acc_ref.shape, F32)

    def scores(j, tile, diagonal, s_buf, cm_buf):
        for mi in range(2):
            kj = ka_ref[mi, j]
            if diagonal:
                s = _nt_dot(kj, qm_ref[0, 2 * tile + mi]) + bias_diag_ref[...]
            else:
                s = _nt_dot(kj, qm_ref[1, 2 * tile + mi])
            s_buf[mi] = s
            cm_buf[mi] = jnp.max(s, axis=0, keepdims=True)

    def accumulate(j, tile, s_buf, cm_buf, shift):
        vtj = vt_ref[j]
        for mi in range(2):
            idx = 2 * tile + mi
            m_old = m_ref[idx]
            m_new = jnp.maximum(m_old, cm_buf[mi] + shift)
            alpha = jnp.exp2(m_old - m_new)
            p = jnp.exp2(s_buf[mi] - (m_new - shift))
            acc_ref[idx] = alpha * acc_ref[idx] + _dot(vtj, p.astype(BF16))
            m_ref[idx] = m_new

    def off_shift(j, tile):
        return slope * ((j - (2 * step + tile)) * T).astype(F32)

    def reach_blocks(tile, diag_max):
        gap = jnp.full((1, T), NEG_BIG, F32)
        for mi in range(2):
            reach = jnp.sqrt(q_sq[tile][mi:mi + 1, :]) * (knorm_ref[mi] * NORM_SLACK)
            gap = jnp.maximum(gap, reach - diag_max[mi])
        x = (jnp.max(gap) + UNDERFLOW_LOG2) / slope
        return jnp.floor(jnp.clip(x / T, 0.0, 1e6)).astype(jnp.int32) + 1

    d0 = 2 * step
    scores(d0, 0, True, s_a, cm_a)
    scores(d0 + 1, 1, True, s_b, cm_b)
    reach1 = reach_blocks(1, cm_b)
    needed = jnp.minimum(jnp.maximum(reach_blocks(0, cm_a), reach1), d0)
    accumulate(d0, 0, s_a, cm_a, 0.0)

    def pair(back, j_b, shift_b):
        j0, j1 = d0 - back, d0 + 1 - back
        scores(j0, 0, False, s_a, cm_a)
        accumulate(j_b, 1, s_b, cm_b, shift_b)
        scores(j1, 1, False, s_b, cm_b)
        accumulate(j0, 0, s_a, cm_a, off_shift(j0, 0))
        return j1, off_shift(j1, 1)

    odd = needed % 2

    @pl.when(odd == 1)
    def _():
        pair(1, d0 + 1, jnp.float32(0.0))

    def body(t, carry):
        j_b, shift_b = carry
        for back in (2 * t + 1 + odd, 2 * t + 2 + odd):
            j_b, shift_b = pair(back, j_b, shift_b)
        return j_b, shift_b

    start = (jnp.where(odd == 1, d0, d0 + 1), jnp.where(odd == 1, off_shift(d0, 1), jnp.float32(0.0)))
    j_b, shift_b = lax.fori_loop(0, needed // 2, body, start)

    @pl.when(reach1 > d0)
    def _():
        scores(0, 1, False, s_a, cm_a)
        accumulate(j_b, 1, s_b, cm_b, shift_b)
        accumulate(0, 1, s_a, cm_a, off_shift(0, 1))

    @pl.when(reach1 <= d0)
    def _():
        accumulate(j_b, 1, s_b, cm_b, shift_b)

    lam = (jnp.exp(jnp.sum(lq1_ref[...] * lk1_ref[...], axis=-1, keepdims=True))
           - jnp.exp(jnp.sum(lq2_ref[...] * lk2_ref[...], axis=-1, keepdims=True)) + lam_init)
    for tile in range(2):
        num = [acc_ref[2 * tile + mi, 0:HEAD_W, :] for mi in range(2)]
        den = [acc_ref[2 * tile + mi, HEAD_W:HEAD_W + 1, :] for mi in range(2)]
        o_t = num[0] * (1.0 / den[0]) - lam * (num[1] * (1.0 / den[1]))
        ms = jnp.mean(o_t * o_t, axis=0, keepdims=True)
        o = (o_t * lax.rsqrt(ms + EPS)).T
        o_ref[tile * T:(tile + 1) * T, :] = (o * gsub_ref[...] * (1.0 - lam_init)).astype(BF16)


def _attention(proj, slopes, lq1, lk1, lq2, lk2, g_subln, bsz, seq, lam_init):
    t = proj.shape[0]
    tq = 2 * T_ATTN
    nq = seq // tq
    vec = lambda n: pl.BlockSpec((1, n), lambda b, h, i: (0, 0))
    return pl.pallas_call(
        functools.partial(_attn_kernel, lam_init=lam_init),
        grid=(bsz, ATTN_HEADS, nq),
        in_specs=[
            pl.BlockSpec(memory_space=pltpu.SMEM),
            pl.BlockSpec((tq, HEAD_W), lambda b, h, i: (b * nq + i, COL_Q // HEAD_W + h)),
            pl.BlockSpec((seq, HEAD_W), lambda b, h, i: (b, COL_K // HEAD_W + h)),
            pl.BlockSpec((seq, HEAD_W), lambda b, h, i: (b, COL_V // HEAD_W + h)),
            vec(ATTN_HEAD_DIM), vec(ATTN_HEAD_DIM), vec(ATTN_HEAD_DIM), vec(ATTN_HEAD_DIM),
            vec(HEAD_W),
        ],
        out_specs=pl.BlockSpec((tq, HEAD_W), lambda b, h, i: (b * nq + i, h)),
        out_shape=jax.ShapeDtypeStruct((t, ATTN_HEADS * HEAD_W), BF16),
        scratch_shapes=[
            pltpu.VMEM((seq // T_ATTN, VT_ROWS, T_ATTN), BF16),
            pltpu.VMEM((2, seq // T_ATTN, T_ATTN, HEAD_W), BF16),
            pltpu.VMEM((4, T_ATTN, HEAD_W), BF16),
            pltpu.VMEM((T_ATTN, T_ATTN), F32),
            pltpu.VMEM((2, 4, T_ATTN, HEAD_W), BF16),
            pltpu.VMEM((2, T_ATTN, T_ATTN), F32),
            pltpu.VMEM((2, T_ATTN, T_ATTN), F32),
            pltpu.VMEM((2, 1, T_ATTN), F32),
            pltpu.VMEM((2, 1, T_ATTN), F32),
            pltpu.VMEM((4, 1, T_ATTN), F32),
            pltpu.VMEM((4, VT_ROWS, T_ATTN), F32),
            pltpu.SMEM((2,), F32),
        ],
        compiler_params=pltpu.CompilerParams(
            dimension_semantics=("parallel", "parallel", "arbitrary"),
            vmem_limit_bytes=VMEM_LIMIT),
        name="diff_attn",
    )(slopes, proj, proj, proj, lq1, lk1, lq2, lk2, g_subln)


def _merge_kernel(y_ref, o_ref, gs_ref, ga_ref, x_ref, bgs_ref, bga_ref, wbs_ref, wba_ref, wo_ref,
                  gn_ref, x1_ref, h2_ref):
    br_ssm = _dot(y_ref[...], wbs_ref[...])
    br_attn = _dot(o_ref[...], wba_ref[...])
    g_ssm = _sigmoid(gs_ref[...].astype(F32) + bgs_ref[...])
    g_attn = _sigmoid(ga_ref[...].astype(F32) + bga_ref[...])
    merged = (g_ssm * br_ssm + g_attn * br_attn).astype(BF16)
    x1 = x_ref[...] + _dot(merged, wo_ref[...])
    x1_ref[...] = x1
    ms = jnp.mean(x1 * x1, axis=-1, keepdims=True)
    h2_ref[...] = (x1 * lax.rsqrt(ms + EPS) * gn_ref[...]).astype(BF16)


def _merge(y_ssm, o_attn, proj, x2, bg_ssm, bg_attn, w_bs, w_ba, w_o, g_mlp):
    t = x2.shape[0]
    rows = lambda n: pl.BlockSpec((TM_MERGE, n), lambda i: (i, 0))
    const = lambda shape: pl.BlockSpec(shape, lambda i: (0, 0))
    return pl.pallas_call(
        _merge_kernel,
        grid=(t // TM_MERGE,),
        in_specs=[
            rows(D_INNER),
            rows(D_MODEL),
            pl.BlockSpec((TM_MERGE, D_MODEL), lambda i: (i, COL_GATE // D_MODEL)),
            pl.BlockSpec((TM_MERGE, D_MODEL), lambda i: (i, COL_GATE // D_MODEL + 1)),
            rows(D_MODEL),
            const((1, D_MODEL)), const((1, D_MODEL)),
            const((D_INNER, D_MODEL)), const((D_MODEL, D_MODEL)), const((D_MODEL, D_MODEL)),
            const((1, D_MODEL)),
        ],
        out_specs=[rows(D_MODEL), rows(D_MODEL)],
        out_shape=[jax.ShapeDtypeStruct((t, D_MODEL), F32), jax.ShapeDtypeStruct((t, D_MODEL), BF16)],
        compiler_params=pltpu.CompilerParams(
            dimension_semantics=("parallel",), vmem_limit_bytes=VMEM_LIMIT),
        name="merge",
    )(y_ssm, o_attn, proj, proj, x2, bg_ssm, bg_attn, w_bs, w_ba, w_o, g_mlp)


def _mlp_kernel(x1_ref, h2_ref, wu_ref, wd_ref, gf_ref, out_ref):
    h2 = h2_ref[...]
    acc = x1_ref[...]
    for c in range(D_FF // FF_CHUNK):
        u = jnp.maximum(_dot(h2, wu_ref[:, c * FF_CHUNK:(c + 1) * FF_CHUNK]), 0.0)
        acc = acc + _dot((u * u).astype(BF16), wd_ref[c * FF_CHUNK:(c + 1) * FF_CHUNK, :])
    ms = jnp.mean(acc * acc, axis=-1, keepdims=True)
    out_ref[...] = acc * lax.rsqrt(ms + EPS) * gf_ref[...]


def _mlp(x1, h2, w_up, w_down, g_final):
    t = x1.shape[0]
    rows = lambda n: pl.BlockSpec((TM_MLP, n), lambda i: (i, 0))
    const = lambda shape: pl.BlockSpec(shape, lambda i: (0, 0), pipeline_mode=pl.Buffered(1))
    return pl.pallas_call(
        _mlp_kernel,
        grid=(t // TM_MLP,),
        in_specs=[rows(D_MODEL), rows(D_MODEL), const((D_MODEL, D_FF)), const((D_FF, D_MODEL)),
                  const((1, D_MODEL))],
        out_specs=rows(D_MODEL),
        out_shape=jax.ShapeDtypeStruct((t, D_MODEL), F32),
        compiler_params=pltpu.CompilerParams(
            dimension_semantics=("parallel",), vmem_limit_bytes=VMEM_LIMIT),
        name="mlp",
    )(x1, h2, w_up, w_down, g_final)


def kernel(x, g_norm_mix, w_in, b_gate, conv_w, conv_b, dt_bias, a_log, d_skip, g_ssm_norm,
           lambda_q1, lambda_k1, lambda_q2, lambda_k2, g_subln, w_br_ssm, w_br_attn, w_out,
           g_norm_mlp, w_up, w_down, g_norm_final):
    bsz, seq, _ = x.shape
    depth = w_in.shape[0]
    x2 = x.reshape(bsz * seq, D_MODEL)

    lane_head = jnp.arange(LANES)[:, None] % SSM_HEADS
    live = jnp.arange(LANES)[:, None] < 3 * SSM_HEADS
    sel = (live & (lane_head == jnp.arange(SSM_HEADS * L_SSD)[None, :] // L_SSD)).astype(BF16)
    sel_wide = (live & (lane_head == jnp.arange(D_INNER)[None, :] // SSM_HEAD_DIM)).astype(BF16)
    head_mask = (jnp.arange(GROUP_W)[None, None, :] // SSM_HEAD_DIM
                 == jnp.arange(HEADS_PER_GROUP)[:, None, None]).astype(BF16)
    slopes = jnp.exp2(-8.0 * jnp.arange(1, ATTN_HEADS + 1, dtype=F32) / ATTN_HEADS)

    for l in range(depth):
        w = w_in[l]
        s_z, s_xbc, s_dt = D_INNER, D_INNER + CONV_DIM, D_INNER + CONV_DIM + SSM_HEADS
        w_slab = jnp.concatenate([w[:, s_z:s_xbc], 0.5 * w[:, :s_z], w[:, s_dt:]], axis=1).astype(BF16)
        rep = LANES // SSM_HEADS
        w_dt = jnp.tile(w[:, s_xbc:s_dt], (1, rep)).astype(BF16)
        dtb = jnp.tile(dt_bias[l], rep).reshape(1, LANES)
        a_row = jnp.tile(-jnp.exp(a_log[l].astype(F32)) * LOG2E, rep).reshape(1, LANES)
        dskip_w = jnp.repeat(d_skip[l], SSM_HEAD_DIM).reshape(1, D_INNER)
        lam_init = 0.8 - 0.6 * math.exp(-0.3 * l)

        proj, dt = _inproj(x2, g_norm_mix[l].reshape(1, D_MODEL), w_slab, w_dt, dtb)
        y_ssm = _ssd(proj, dt, (0.5 * conv_w[l]).reshape(CONV_WIDTH, N_SLABS, 1, LANES),
                     (0.5 * conv_b[l]).reshape(N_SLABS, 1, LANES), a_row, dskip_w,
                     g_ssm_norm[l].reshape(1, D_INNER), sel, sel_wide, head_mask, bsz, seq)
        o_attn = _attention(proj, slopes,
                            lambda_q1[l].reshape(1, -1), lambda_k1[l].reshape(1, -1),
                            lambda_q2[l].reshape(1, -1), lambda_k2[l].reshape(1, -1),
                            g_subln[l].reshape(1, HEAD_W), bsz, seq, lam_init)
        x1, h2 = _merge(y_ssm, o_attn, proj, x2,
                        b_gate[l, :D_MODEL].reshape(1, D_MODEL), b_gate[l, D_MODEL:].reshape(1, D_MODEL),
                        w_br_ssm[l].astype(BF16), w_br_attn[l].astype(BF16), w_out[l].astype(BF16),
                        g_norm_mlp[l].reshape(1, D_MODEL))
        assert l == depth - 1
        x2 = _mlp(x1, h2, w_up[l].astype(BF16), w_down[l].astype(BF16),
                  g_norm_final.reshape(1, D_MODEL))
    return x2.reshape(bsz, seq, D_MODEL)
```
